```python
import numpy as np
import jax
import jax.numpy as jnp
from jax import lax

D_MODEL = 4096
BATCH = 2
SEQ = 4096
DEPTH = 2

MIX_WIDTH = D_MODEL
HEAD_DIM = 128
CONV_WIDTH = D_MODEL // 4
FOX_WIDTH = (MIX_WIDTH - CONV_WIDTH) // 2
RET_WIDTH = MIX_WIDTH - CONV_WIDTH - FOX_WIDTH
FOX_HEADS = FOX_WIDTH // HEAD_DIM
RET_HEADS = RET_WIDTH // HEAD_DIM
CONV_K = 3
D_FF = 4 * D_MODEL
Q_BLOCK = 128
RET_CHUNK = 128
ROPE_BASE = 10000.0
LN_EPS = 1e-5
GN_EPS = 1e-6
ALPHA = (2.0 * DEPTH) ** 0.25
BETA = (8.0 * DEPTH) ** -0.25
PROJ_SIZES = (CONV_WIDTH,) * 3 + (FOX_WIDTH,) * 3 + (FOX_HEADS,) + (RET_WIDTH,) * 4
IN_WIDTH = sum(PROJ_SIZES)

kernel_name = 'hybrid_conv_fox_retention_block'


def layer_norm(x, gain, bias):
    xf = x.astype(jnp.float32)
    mu = jnp.mean(xf, axis=-1, keepdims=True)
    var = jnp.mean(jnp.square(xf - mu), axis=-1, keepdims=True)
    return ((xf - mu) * lax.rsqrt(var + LN_EPS)).astype(x.dtype) * gain + bias


def rotary(t, cos, sin):
    t1, t2 = jnp.split(t, 2, axis=-1)
    return jnp.concatenate([t1 * cos - t2 * sin, t1 * sin + t2 * cos], axis=-1)


def short_conv_mixer(u, gate_b, gate_c, conv_w):
    z = gate_c * u
    y = lax.conv_general_dilated(
        z, conv_w[:, None, :].astype(z.dtype), window_strides=(1,),
        padding=[(CONV_K - 1, 0)], dimension_numbers=('NWC', 'WIO', 'NWC'),
        feature_group_count=z.shape[-1])
    return gate_b * y


def forgetting_attention(q, k, v, f_logit, b_f):
    B, S, _ = q.shape
    H, Dh, QB = FOX_HEADS, HEAD_DIM, Q_BLOCK
    nb = S // QB

    def heads(t):
        return t.reshape(B, S, H, Dh).transpose(0, 2, 1, 3)

    q, k, v = heads(q), heads(k), heads(v)
    log_f = jax.nn.log_sigmoid(f_logit.astype(jnp.float32) + b_f.astype(jnp.float32))
    cum = jnp.cumsum(log_f, axis=1).transpose(0, 2, 1)
    q_blocks = q.reshape(B, H, nb, QB, Dh).transpose(2, 0, 1, 3, 4)
    cum_blocks = cum.reshape(B, H, nb, QB).transpose(2, 0, 1, 3)
    k_pos = jnp.arange(S)
    scale = Dh ** -0.5

    def block(args):
        qb, cb, i = args
        logits = (jnp.einsum('bhqd,bhkd->bhqk', qb, k).astype(jnp.float32) * scale
                  + (cb[..., :, None] - cum[:, :, None, :]))
        q_pos = i * QB + jnp.arange(QB)
        logits = jnp.where(k_pos[None, :] <= q_pos[:, None], logits, -jnp.inf)
        p = jax.nn.softmax(logits, axis=-1)
        return jnp.einsum('bhqk,bhkd->bhqd', p.astype(v.dtype), v)

    out = lax.map(block, (q_blocks, cum_blocks, jnp.arange(nb)))
    return out.transpose(1, 0, 3, 2, 4).reshape(B, S, H * Dh)


def retention(q, k, v, g):
    B, S, _ = q.shape
    H, Dh, C = RET_HEADS, HEAD_DIM, RET_CHUNK
    nc = S // C
    f32 = jnp.float32

    def heads(t):
        return t.astype(f32).reshape(B, S, H, Dh).transpose(0, 2, 1, 3)

    q, k, v = heads(q), heads(k), heads(v)
    pos = jnp.arange(S, dtype=f32)
    inv_freq = ROPE_BASE ** (-jnp.arange(0, Dh, 2, dtype=f32) / Dh)
    ang = pos[:, None] * inv_freq[None, :]
    cos, sin = jnp.cos(ang), jnp.sin(ang)
    q = rotary(q, cos, sin)
    k = rotary(k, cos, sin) * (Dh ** -0.5)

    log_gamma = jnp.log1p(-jnp.exp2(-5.0 - jnp.arange(H, dtype=f32)))
    j = jnp.arange(C, dtype=f32)
    diff = j[:, None] - j[None, :]
    intra = jnp.where(diff >= 0, jnp.exp(jnp.maximum(diff, 0.0) * log_gamma[:, None, None]), 0.0)
    q_decay = jnp.exp((j + 1.0) * log_gamma[:, None])[..., None]
    k_decay = jnp.exp((C - 1.0 - j) * log_gamma[:, None])[..., None]
    chunk_decay = jnp.exp(C * log_gamma)[:, None, None]

    def chunks(t):
        return t.reshape(B, H, nc, C, Dh).transpose(2, 0, 1, 3, 4)

    def step(state, qkv):
        qc, kc, vc = qkv
        scores = jnp.einsum('bhqd,bhkd->bhqk', qc, kc) * intra
        out = (jnp.einsum('bhqk,bhkd->bhqd', scores, vc)
               + jnp.einsum('bhqd,bhde->bhqe', qc * q_decay, state))
        state = state * chunk_decay + jnp.einsum('bhkd,bhke->bhde', kc * k_decay, vc)
        return state, out

    state0 = jnp.zeros((B, H, Dh, Dh), f32)
    _, out = lax.scan(step, state0, (chunks(q), chunks(k), chunks(v)))
    y = out.transpose(1, 0, 3, 2, 4).reshape(B, S, H, Dh)
    mu = jnp.mean(y, axis=-1, keepdims=True)
    var = jnp.mean(jnp.square(y - mu), axis=-1, keepdims=True)
    y = ((y - mu) * lax.rsqrt(var + GN_EPS)).reshape(B, S, H * Dh)
    return y.astype(g.dtype) * jax.nn.silu(g)


def hybrid_mixer(h, w_in, b_f, conv_w, w_out):
    proj = jnp.einsum('bsd,de->bse', h, w_in)
    split_at = np.cumsum(PROJ_SIZES)[:-1].tolist()
    (u, gate_b, gate_c, fq, fk, fv, f_logit, rq, rk, rv, rg) = jnp.split(proj, split_at, axis=-1)
    y = jnp.concatenate([
        short_conv_mixer(u, gate_b, gate_c, conv_w),
        forgetting_attention(fq, fk, fv, f_logit, b_f),
        retention(rq, rk, rv, rg),
    ], axis=-1)
    return jnp.einsum('bse,ed->bsd', y, w_out)


def squared_relu_mlp(h, w_up, w_down):
    a = jnp.square(jax.nn.relu(jnp.einsum('bsd,df->bsf', h, w_up)))
    return jnp.einsum('bsf,fd->bsd', a, w_down)


def setup_inputs(seed: int = 0) -> dict:
    key = jax.random.key(seed)
    ks = jax.random.split(key, 14)
    L, D = DEPTH, D_MODEL

    def nrm(k, shape, scale):
        return jax.random.normal(k, shape, jnp.float32) * scale

    return {
        'x': nrm(ks[0], (BATCH, SEQ, D), 1.0),
        'c': nrm(ks[1], (BATCH, D), 1.0),
        'w_ada': nrm(ks[2], (L, D, 6 * D), 0.5 * D ** -0.5),
        'b_ada': nrm(ks[3], (L, 6 * D), 0.02),
        'w_in': nrm(ks[4], (L, D, IN_WIDTH), D ** -0.5),
        'b_f': 2.0 + nrm(ks[5], (L, FOX_HEADS), 0.5),
        'conv_w': nrm(ks[6], (L, CONV_K, CONV_WIDTH), CONV_K ** -0.5),
        'w_out': nrm(ks[7], (L, MIX_WIDTH, D), BETA * MIX_WIDTH ** -0.5),
        'ln1_g': 1.0 + nrm(ks[8], (L, D), 0.02),
        'ln1_b': nrm(ks[9], (L, D), 0.02),
        'w_up': nrm(ks[10], (L, D, D_FF), D ** -0.5),
        'w_down': nrm(ks[11], (L, D_FF, D), BETA * D_FF ** -0.5),
        'ln2_g': 1.0 + nrm(ks[12], (L, D), 0.02),
        'ln2_b': nrm(ks[13], (L, D), 0.02),
    }


def reference(x, c, w_ada, b_ada, w_in, b_f, conv_w, w_out, ln1_g, ln1_b, w_up, w_down, ln2_g, ln2_b):
    cond = jax.nn.silu(c)
    for l in range(DEPTH):
        mod = jnp.einsum('bd,de->be', cond, w_ada[l]) + b_ada[l]
        sh_a, sc_a, g_a, sh_m, sc_m, g_m = jnp.split(mod[:, None, :], 6, axis=-1)
        h = x * (1 + sc_a) + sh_a
        x = layer_norm(ALPHA * x + g_a * hybrid_mixer(h, w_in[l], b_f[l], conv_w[l], w_out[l]),
                       ln1_g[l], ln1_b[l])
        h = x * (1 + sc_m) + sh_m
        x = layer_norm(ALPHA * x + g_m * squared_relu_mlp(h, w_up[l], w_down[l]),
                       ln2_g[l], ln2_b[l])
    return x
```

```python
import functools

import numpy as np
import jax
import jax.numpy as jnp
from jax import lax
from jax.experimental import pallas as pl
from jax.experimental.pallas import tpu as pltpu

D_MODEL = 4096
HEAD_DIM = 128
CONV_WIDTH = D_MODEL // 4
FOX_WIDTH = (D_MODEL - CONV_WIDTH) // 2
RET_WIDTH = D_MODEL - CONV_WIDTH - FOX_WIDTH
FOX_HEADS = FOX_WIDTH // HEAD_DIM
RET_HEADS = RET_WIDTH // HEAD_DIM
CONV_K = 3
ROPE_BASE = 10000.0
LN_EPS = 1e-5
GN_EPS = 1e-6

FOX_OFF = 3 * CONV_WIDTH
FLOGIT_OFF = FOX_OFF + 3 * FOX_WIDTH
RET_OFF = FLOGIT_OFF + FOX_HEADS
MAIN_WIDTH = 3 * CONV_WIDTH + 3 * FOX_WIDTH + 4 * RET_WIDTH
RET_MAIN_OFF = FOX_OFF + 3 * FOX_WIDTH

LANE = 128
VMEM_LIMIT = 56 * 1024 * 1024

BF16 = jnp.bfloat16
F32 = jnp.float32


def _params(*sem):
    return pltpu.CompilerParams(dimension_semantics=sem, vmem_limit_bytes=VMEM_LIMIT)


def _ada_kernel(c_ref, w_ref, b_ref, o_ref):
    c = c_ref[...]
    cond = (c / (1.0 + jnp.exp(-c))).astype(BF16)
    acc = jnp.dot(cond, w_ref[...].astype(BF16), preferred_element_type=F32)
    o_ref[...] = acc + b_ref[...]


def _ada(c_pad, w_ada, b_ada, tn=512):
    depth, d, n = w_ada.shape
    rows = c_pad.shape[0]
    return pl.pallas_call(
        _ada_kernel,
        grid=(depth, n // tn),
        in_specs=[
            pl.BlockSpec((rows, d), lambda l, j: (0, 0)),
            pl.BlockSpec((None, d, tn), lambda l, j: (l, 0, j)),
            pl.BlockSpec((None, 1, tn), lambda l, j: (l, 0, j)),
        ],
        out_specs=pl.BlockSpec((None, rows, tn), lambda l, j: (l, 0, j)),
        out_shape=jax.ShapeDtypeStruct((depth, rows, n), F32),
        compiler_params=_params("parallel", "parallel"),
        name="ada_modulation",
    )(c_pad, w_ada, b_ada.reshape(depth, 1, n))


def _modulate_kernel(x_ref, sc_ref, sh_ref, h_ref):
    h_ref[...] = (x_ref[...] * (1.0 + sc_ref[...]) + sh_ref[...]).astype(h_ref.dtype)


def _modulate(x2d, sc, sh, seq, tm=512):
    m, d = x2d.shape
    per_b = seq // tm
    vec = pl.BlockSpec((None, 1, d), lambda i: (i // per_b, 0, 0))
    return pl.pallas_call(
        _modulate_kernel,
        grid=(m // tm,),
        in_specs=[pl.BlockSpec((tm, d), lambda i: (i, 0)), vec, vec],
        out_specs=pl.BlockSpec((tm, d), lambda i: (i, 0)),
        out_shape=jax.ShapeDtypeStruct((m, d), BF16),
        compiler_params=_params("parallel"),
        name="modulate",
    )(x2d, sc, sh)


def _mm_kernel(a_ref, w_ref, o_ref, *scratch, nk, relu2):
    part = jnp.dot(a_ref[...], w_ref[...], preferred_element_type=F32)

    def finish(acc):
        if relu2:
            acc = jnp.square(jnp.maximum(acc, 0.0))
        o_ref[...] = acc.astype(o_ref.dtype)

    if nk == 1:
        finish(part)
        return
    acc_ref, = scratch
    k = pl.program_id(2)

    @pl.when(k == 0)
    def _():
        acc_ref[...] = part

    @pl.when(jnp.logical_and(k > 0, k < nk - 1))
    def _():
        acc_ref[...] += part

    @pl.when(k == nk - 1)
    def _():
        finish(acc_ref[...] + part)


def _matmul(a, w, *, tm, tn, tk, out_dtype, relu2=False, name="matmul"):
    m, kdim = a.shape
    _, n = w.shape
    nk = kdim // tk
    scratch = [] if nk == 1 else [pltpu.VMEM((tm, tn), F32)]
    return pl.pallas_call(
        functools.partial(_mm_kernel, nk=nk, relu2=relu2),
        grid=(m // tm, n // tn, nk),
        in_specs=[
            pl.BlockSpec((tm, tk), lambda i, j, k: (i, k)),
            pl.BlockSpec((tk, tn), lambda i, j, k: (k, j)),
        ],
        out_specs=pl.BlockSpec((tm, tn), lambda i, j, k: (i, j)),
        out_shape=jax.ShapeDtypeStruct((m, n), out_dtype),
        scratch_shapes=scratch,
        compiler_params=_params("parallel", "parallel", "arbitrary"),
        name=name,
    )(a, w)


def _conv_kernel(u_ref, b_ref, c_ref, w_ref, o_ref, *, rows):
    seq, tc = u_ref.shape
    w0 = w_ref[0:1, :]
    w1 = w_ref[1:2, :]
    w2 = w_ref[2:3, :]
    rid = lax.broadcasted_iota(jnp.int32, (rows, 1), 0)

    def chunk(ci, carry):
        prev2, prev1 = carry
        r0 = pl.multiple_of(ci * rows, rows)
        z = c_ref[pl.ds(r0, rows), :].astype(F32) * u_ref[pl.ds(r0, rows), :].astype(F32)
        z1 = jnp.where(rid == 0, prev1, pltpu.roll(z, 1, 0))
        z2 = jnp.where(rid == 0, prev2, jnp.where(rid == 1, prev1, pltpu.roll(z, 2, 0)))
        y = w0 * z2 + w1 * z1 + w2 * z
        o_ref[pl.ds(r0, rows), :] = (b_ref[pl.ds(r0, rows), :].astype(F32) * y).astype(o_ref.dtype)
        return z[rows - 2:rows - 1, :], z[rows - 1:rows, :]

    zero = jnp.zeros((1, tc), F32)
    lax.fori_loop(0, seq // rows, chunk, (zero, zero))


def _short_conv(proj, conv_w, batch, seq, tc=256, rows=512):
    nblk = CONV_WIDTH // tc
    return pl.pallas_call(
        functools.partial(_conv_kernel, rows=rows),
        grid=(batch, nblk),
        in_specs=[
            pl.BlockSpec((seq, tc), lambda b, j: (b, j)),
            pl.BlockSpec((seq, tc), lambda b, j: (b, nblk + j)),
            pl.BlockSpec((seq, tc), lambda b, j: (b, 2 * nblk + j)),
            pl.BlockSpec((CONV_K, tc), lambda b, j: (0, j)),
        ],
        out_specs=pl.BlockSpec((seq, tc), lambda b, j: (b, j)),
        out_shape=jax.ShapeDtypeStruct((batch * seq, CONV_WIDTH), BF16),
        compiler_params=_params("parallel", "parallel"),
        name="short_conv",
    )(proj, proj, proj, conv_w)


def _cum_kernel(f_ref, bf_ref, o_ref, carry_ref):
    rows = f_ref.shape[0]

    @pl.when(pl.program_id(1) == 0)
    def _():
        carry_ref[...] = jnp.zeros_like(carry_ref)

    z = f_ref[...] + bf_ref[...]
    logf = jnp.minimum(z, 0.0) - jnp.log1p(jnp.exp(-jnp.abs(z)))
    p1 = logf.astype(BF16)
    r1 = logf - p1.astype(F32)
    p2 = r1.astype(BF16)
    p3 = (r1 - p2.astype(F32)).astype(BF16)
    ri = lax.broadcasted_iota(jnp.int32, (rows, rows), 0)
    ci = lax.broadcasted_iota(jnp.int32, (rows, rows), 1)
    tri = (ri >= ci).astype(BF16)
    cum = (jnp.dot(tri, p1, preferred_element_type=F32)
           + jnp.dot(tri, p2, preferred_element_type=F32)
           + jnp.dot(tri, p3, preferred_element_type=F32)) + carry_ref[...]
    o_ref[...] = cum
    carry_ref[...] = cum[rows - 1:rows, :]


def _forget_cumsum(f_logit, b_f_pad, batch, seq, rows=512):
    per_b = seq // rows
    return pl.pallas_call(
        _cum_kernel,
        grid=(batch, per_b),
        in_specs=[
            pl.BlockSpec((rows, LANE), lambda b, j: (b * per_b + j, 0)),
            pl.BlockSpec((1, LANE), lambda b, j: (0, 0)),
        ],
        out_specs=pl.BlockSpec((rows, LANE), lambda b, j: (b * per_b + j, 0)),
        out_shape=jax.ShapeDtypeStruct((batch * seq, LANE), F32),
        scratch_shapes=[pltpu.VMEM((1, LANE), F32)],
        compiler_params=_params("parallel", "arbitrary"),
        name="forget_cumsum",
    )(f_logit, b_f_pad)


def _fox_kernel(q_ref, k_ref, v_ref, cq_ref, ck_ref, o_ref, m_ref, l_ref, acc_ref, *, blk):
    seq = q_ref.shape[0]
    scale = HEAD_DIM ** -0.5
    ri = lax.broadcasted_iota(jnp.int32, (blk, blk), 0)
    ci = lax.broadcasted_iota(jnp.int32, (blk, blk), 1)
    causal = ci <= ri

    def q_block(qi, _):
        q0 = pl.multiple_of(qi * blk, blk)
        q = q_ref[pl.ds(q0, blk), :]
        cq = cq_ref[pl.ds(q0, blk), :]
        m_ref[...] = jnp.full_like(m_ref, -jnp.inf)
        l_ref[...] = jnp.zeros_like(l_ref)
        acc_ref[...] = jnp.zeros_like(acc_ref)

        def kv_block(ki, masked):
            k0 = pl.multiple_of(ki * blk, blk)
            k = k_ref[pl.ds(k0, blk), :]
            s = lax.dot_general(q, k, (((1,), (1,)), ((), ())), preferred_element_type=F32)
            s = s * scale + (cq - ck_ref[ki])
            if masked:
                s = jnp.where(causal, s, -jnp.inf)
            m_old = m_ref[...]
            m_new = jnp.maximum(m_old, jnp.max(s, axis=1, keepdims=True))
            p = jnp.exp(s - m_new)
            alpha = jnp.exp(m_old - m_new)
            l_ref[...] = alpha * l_ref[...] + jnp.sum(p, axis=1, keepdims=True)
            acc_ref[...] = alpha * acc_ref[...] + jnp.dot(
                p.astype(BF16), v_ref[pl.ds(k0, blk), :], preferred_element_type=F32)
            m_ref[...] = m_new

        def full_block(ki, carry):
            kv_block(ki, False)
            return carry

        lax.fori_loop(0, qi, full_block, 0)
        kv_block(qi, True)
        o_ref[pl.ds(q0, blk), :] = (acc_ref[...] / l_ref[...]).astype(o_ref.dtype)
        return 0

    lax.fori_loop(0, seq // blk, q_block, 0)


def _forgetting_attention(proj, cum_col, cum_row, batch, seq, blk=512):
    heads = FOX_HEADS
    qoff = FOX_OFF // HEAD_DIM

    def head_spec(off):
        return pl.BlockSpec((seq, HEAD_DIM), lambda b, h: (b, off + h))

    return pl.pallas_call(
        functools.partial(_fox_kernel, blk=blk),
        grid=(batch, heads),
        in_specs=[
            head_spec(qoff), head_spec(qoff + heads), head_spec(qoff + 2 * heads),
            pl.BlockSpec((None, seq, 1), lambda b, h: (b * heads + h, 0, 0)),
            pl.BlockSpec((None, seq // blk, 1, blk), lambda b, h: (b * heads + h, 0, 0, 0)),
        ],
        out_specs=pl.BlockSpec((seq, HEAD_DIM), lambda b, h: (b, h)),
        out_shape=jax.ShapeDtypeStruct((batch * seq, FOX_WIDTH), BF16),
        scratch_shapes=[
            pltpu.VMEM((blk, 1), F32), pltpu.VMEM((blk, 1), F32), pltpu.VMEM((blk, HEAD_DIM), F32),
        ],
        compiler_params=_params("parallel", "parallel"),
        name="forgetting_attention",
    )(proj, proj, proj, cum_col, cum_row)


def _ret_kernel(lg_ref, q_ref, k_ref, v_ref, g_ref, cos_ref, sin_ref, o_ref, *, chunk):
    seq = q_ref.shape[0]
    lg = lg_ref[pl.program_id(1)]
    scale = HEAD_DIM ** -0.5
    ri = lax.broadcasted_iota(jnp.int32, (chunk, chunk), 0)
    ci = lax.broadcasted_iota(jnp.int32, (chunk, chunk), 1)
    diff = (ri - ci).astype(F32)
    intra = jnp.where(diff >= 0.0, jnp.exp(jnp.maximum(diff, 0.0) * lg), 0.0)
    jv = lax.broadcasted_iota(jnp.int32, (chunk, 1), 0).astype(F32)
    q_decay = jnp.exp((jv + 1.0) * lg)
    k_decay = jnp.exp((chunk - 1.0 - jv) * lg)
    chunk_decay = jnp.exp(jnp.full((1, 1), float(chunk), F32) * lg)
    half = HEAD_DIM // 2

    def rotate(t, cosf, sinf):
        return t * cosf + pltpu.roll(t, half, 1) * sinf

    def step(c, state):
        r0 = pl.multiple_of(c * chunk, chunk)
        rows = pl.ds(r0, chunk)
        cosf = cos_ref[rows, :]
        sinf = sin_ref[rows, :]
        q = rotate(q_ref[rows, :].astype(F32), cosf, sinf)
        k = rotate(k_ref[rows, :].astype(F32), cosf, sinf) * scale
        v = v_ref[rows, :]
        scores = lax.dot_general(q.astype(BF16), k.astype(BF16), (((1,), (1,)), ((), ())),
                                 preferred_element_type=F32) * intra
        out = (jnp.dot(scores.astype(BF16), v, preferred_element_type=F32)
               + jnp.dot((q * q_decay).astype(BF16), state.astype(BF16), preferred_element_type=F32))
        state = state * chunk_decay + lax.dot_general(
            (k * k_decay).astype(BF16), v, (((0,), (0,)), ((), ())), preferred_element_type=F32)
        mu = jnp.mean(out, axis=1, keepdims=True)
        cen = out - mu
        var = jnp.mean(cen * cen, axis=1, keepdims=True)
        y = cen * lax.rsqrt(var + GN_EPS)
        g = g_ref[rows, :].astype(F32)
        o_ref[rows, :] = (y * (g / (1.0 + jnp.exp(-g)))).astype(o_ref.dtype)
        return state

    lax.fori_loop(0, seq // chunk, step, jnp.zeros((HEAD_DIM, HEAD_DIM), F32))


def _retention(proj, log_gamma, cosf, sinf, batch, seq, chunk=256):
    heads = RET_HEADS
    qoff = RET_MAIN_OFF // HEAD_DIM

    def head_spec(off):
        return pl.BlockSpec((seq, HEAD_DIM), lambda b, h, lg: (b, off + h))

    table = pl.BlockSpec((seq, HEAD_DIM), lambda b, h, lg: (0, 0))
    return pl.pallas_call(
        functools.partial(_ret_kernel, chunk=chunk),
        grid_spec=pltpu.PrefetchScalarGridSpec(
            num_scalar_prefetch=1,
            grid=(batch, heads),
            in_specs=[head_spec(qoff), head_spec(qoff + heads), head_spec(qoff + 2 * heads),
                      head_spec(qoff + 3 * heads), table, table],
            out_specs=pl.BlockSpec((seq, HEAD_DIM), lambda b, h, lg: (b, h)),
        ),
        out_shape=jax.ShapeDtypeStruct((batch * seq, RET_WIDTH), BF16),
        compiler_params=_params("parallel", "parallel"),
        name="retention",
    )(log_gamma, proj, proj, proj, proj, cosf, sinf)


def _ln_kernel(x_ref, y_ref, g_ref, gain_ref, bias_ref, *rest, alpha, modulate):
    r = alpha * x_ref[...] + g_ref[...] * y_ref[...]
    mu = jnp.mean(r, axis=1, keepdims=True)
    cen = r - mu
    var = jnp.mean(cen * cen, axis=1, keepdims=True)
    xn = cen * lax.rsqrt(var + LN_EPS) * gain_ref[...] + bias_ref[...]
    if modulate:
        sc_ref, sh_ref, xo_ref, h_ref = rest
        xo_ref[...] = xn
        h_ref[...] = (xn * (1.0 + sc_ref[...]) + sh_ref[...]).astype(h_ref.dtype)
    else:
        xo_ref, = rest
        xo_ref[...] = xn


def _residual_ln(x2d, branch, gate, gain, bias, sc, sh, seq, alpha, tm=256):
    m, d = x2d.shape
    per_b = seq // tm
    modulate = sc is not None
    row = pl.BlockSpec((tm, d), lambda i: (i, 0))
    vec_b = pl.BlockSpec((None, 1, d), lambda i: (i // per_b, 0, 0))
    vec = pl.BlockSpec((1, d), lambda i: (0, 0))
    in_specs = [row, row, vec_b, vec, vec]
    args = [x2d, branch, gate, gain, bias]
    out_specs = [row]
    out_shape = [jax.ShapeDtypeStruct((m, d), F32)]
    if modulate:
        in_specs += [vec_b, vec_b]
        args += [sc, sh]
        out_specs.append(row)
        out_shape.append(jax.ShapeDtypeStruct((m, d), BF16))
    res = pl.pallas_call(
        functools.partial(_ln_kernel, alpha=alpha, modulate=modulate),
        grid=(m // tm,),
        in_specs=in_specs,
        out_specs=out_specs,
        out_shape=out_shape,
        compiler_params=_params("parallel"),
        name="residual_layernorm",
    )(*args)
    return (res[0], res[1]) if modulate else (res[0], None)


def _rotary_tables(seq):
    pos = np.arange(seq, dtype=np.float32)
    inv_freq = (np.float32(ROPE_BASE) ** (-np.arange(0, HEAD_DIM, 2, dtype=np.float32) / np.float32(HEAD_DIM))
                ).astype(np.float32)
    ang = (pos[:, None] * inv_freq[None, :]).astype(np.float32)
    cos = np.cos(ang).astype(np.float32)
    sin = np.sin(ang).astype(np.float32)
    return (jnp.asarray(np.concatenate([cos, cos], axis=1)),
            jnp.asarray(np.concatenate([-sin, sin], axis=1)))


def _log_gamma():
    h = np.arange(RET_HEADS, dtype=np.float32)
    return jnp.asarray(np.log1p(-np.exp2(np.float32(-5.0) - h)).astype(np.float32))


def kernel(x, c, w_ada, b_ada, w_in, b_f, conv_w, w_out, ln1_g, ln1_b, w_up, w_down, ln2_g, ln2_b):
    batch, seq, d = x.shape
    depth = w_ada.shape[0]
    m = batch * seq
    alpha = (2.0 * depth) ** 0.25
    pad_rows = 8

    c_pad = jnp.zeros((pad_rows, d), F32).at[:batch].set(c)
    mod = _ada(c_pad, w_ada, b_ada)
    mod = mod[:, :batch].reshape(depth, batch, 6, 1, d)

    cosf, sinf = _rotary_tables(seq)
    log_gamma = _log_gamma()

    x2d = x.reshape(m, d)
    h = _modulate(x2d, mod[0, :, 1], mod[0, :, 0], seq)
    for l in range(depth):
        sh_m, sc_m, g_a, g_m = mod[l, :, 3], mod[l, :, 4], mod[l, :, 2], mod[l, :, 5]
        w_in_l = w_in[l]
        w_main = jnp.concatenate([w_in_l[:, :FLOGIT_OFF], w_in_l[:, RET_OFF:]], axis=1).astype(BF16)
        w_f = jnp.zeros((d, LANE), BF16).at[:, :FOX_HEADS].set(
            w_in_l[:, FLOGIT_OFF:RET_OFF].astype(BF16))
        b_f_pad = jnp.zeros((1, LANE), F32).at[0, :FOX_HEADS].set(b_f[l])

        proj = _matmul(h, w_main, tm=1024, tn=1536, tk=d, out_dtype=BF16, name="in_proj")
        f_logit = _matmul(h, w_f, tm=1024, tn=LANE, tk=d, out_dtype=F32, name="forget_logits")

        cum = _forget_cumsum(f_logit, b_f_pad, batch, seq)
        cum_bh = cum[:, :FOX_HEADS].reshape(batch, seq, FOX_HEADS).transpose(0, 2, 1)
        cum_bh = cum_bh.reshape(batch * FOX_HEADS, seq)
        y_conv = _short_conv(proj, conv_w[l], batch, seq)
        fox_blk = 512
        y_fox = _forgetting_attention(
            proj, cum_bh[:, :, None], cum_bh.reshape(batch * FOX_HEADS, seq // fox_blk, 1, fox_blk),
            batch, seq, blk=fox_blk)
        y_ret = _retention(proj, log_gamma, cosf, sinf, batch, seq)
        y = jnp.concatenate([y_conv, y_fox, y_ret], axis=1)

        mix = _matmul(y, w_out[l].astype(BF16), tm=1024, tn=1024, tk=d, out_dtype=F32, name="out_proj")
        x2d, h2 = _residual_ln(x2d, mix, g_a, ln1_g[l][None], ln1_b[l][None], sc_m, sh_m, seq, alpha)

        a = _matmul(h2, w_up[l].astype(BF16), tm=1024, tn=1024, tk=d, out_dtype=BF16, relu2=True,
                    name="mlp_up")
        mlp = _matmul(a, w_down[l].astype(BF16), tm=1024, tn=1024, tk=d, out_dtype=F32, name="mlp_down")
        if l + 1 < depth:
            x2d, h = _residual_ln(x2d, mlp, g_m, ln2_g[l][None], ln2_b[l][None],
                                  mod[l + 1, :, 1], mod[l + 1, :, 0], seq, alpha)
        else:
            x2d, _ = _residual_ln(x2d, mlp, g_m, ln2_g[l][None], ln2_b[l][None], None, None, seq, alpha)
    return x2d.reshape(batch, seq, d)
```

```python
import functools

import numpy as np
import jax
import jax.numpy as jnp
from jax import lax
from jax.experimental import pallas as pl
from jax.experimental.pallas import tpu as pltpu

D_MODEL = 4096
HEAD_DIM = 128
CONV_WIDTH = D_MODEL // 4
FOX_WIDTH = (D_MODEL - CONV_WIDTH) // 2
RET_WIDTH = D_MODEL - CONV_WIDTH - FOX_WIDTH
FOX_HEADS = FOX_WIDTH // HEAD_DIM
RET_HEADS = RET_WIDTH // HEAD_DIM
CONV_K = 3
ROPE_BASE = 10000.0
LN_EPS = 1e-5
GN_EPS = 1e-6

FOX_OFF = 3 * CONV_WIDTH
FLOGIT_OFF = FOX_OFF + 3 * FOX_WIDTH
RET_OFF = FLOGIT_OFF + FOX_HEADS
MAIN_WIDTH = 3 * CONV_WIDTH + 3 * FOX_WIDTH + 4 * RET_WIDTH
RET_MAIN_OFF = FOX_OFF + 3 * FOX_WIDTH

LANE = 128
LOG2_E = 1.4426950408889634
VMEM_LIMIT = 56 * 1024 * 1024

BF16 = jnp.bfloat16
F32 = jnp.float32


def _params(*sem):
    return pltpu.CompilerParams(dimension_semantics=sem, vmem_limit_bytes=VMEM_LIMIT)


def _ada_kernel(c_ref, w_ref, b_ref, o_ref):
    c = c_ref[...]
    cond = (c / (1.0 + jnp.exp(-c))).astype(BF16)
    acc = jnp.dot(cond, w_ref[...].astype(BF16), preferred_element_type=F32)
    o_ref[...] = acc + b_ref[...]


def _ada(c_pad, w_ada, b_ada, tn=512):
    depth, d, n = w_ada.shape
    rows = c_pad.shape[0]
    return pl.pallas_call(
        _ada_kernel,
        grid=(depth, n // tn),
        in_specs=[
            pl.BlockSpec((rows, d), lambda l, j: (0, 0)),
            pl.BlockSpec((None, d, tn), lambda l, j: (l, 0, j)),
            pl.BlockSpec((None, 1, tn), lambda l, j: (l, 0, j)),
        ],
        out_specs=pl.BlockSpec((None, rows, tn), lambda l, j: (l, 0, j)),
        out_shape=jax.ShapeDtypeStruct((depth, rows, n), F32),
        compiler_params=_params("parallel", "parallel"),
        name="ada_modulation",
    )(c_pad, w_ada, b_ada.reshape(depth, 1, n))


def _modulate_kernel(x_ref, sc_ref, sh_ref, h_ref):
    h_ref[...] = (x_ref[...] * (1.0 + sc_ref[...]) + sh_ref[...]).astype(h_ref.dtype)


def _modulate(x2d, sc, sh, seq, tm=512):
    m, d = x2d.shape
    per_b = seq // tm
    vec = pl.BlockSpec((None, 1, d), lambda i: (i // per_b, 0, 0))
    return pl.pallas_call(
        _modulate_kernel,
        grid=(m // tm,),
        in_specs=[pl.BlockSpec((tm, d), lambda i: (i, 0)), vec, vec],
        out_specs=pl.BlockSpec((tm, d), lambda i: (i, 0)),
        out_shape=jax.ShapeDtypeStruct((m, d), BF16),
        compiler_params=_params("parallel"),
        name="modulate",
    )(x2d, sc, sh)


def _repack_kernel(a_ref, b_ref, o_ref, *, aligned_tiles, shift):
    j = pl.program_id(2)
    ntile = o_ref.shape[1] // LANE

    @pl.when(j < aligned_tiles)
    def _():
        o_ref[...] = a_ref[...].astype(o_ref.dtype)

    @pl.when(j >= aligned_tiles)
    def _():
        keep = lax.broadcasted_iota(jnp.int32, (1, LANE), 1) < (LANE - shift)
        prev = pltpu.roll(a_ref[:, 0:LANE], LANE - shift, 1)
        for t in range(ntile):
            src = a_ref[:, (t + 1) * LANE:(t + 2) * LANE] if t + 1 < ntile else b_ref[...]
            nxt = pltpu.roll(src, LANE - shift, 1)
            o_ref[:, t * LANE:(t + 1) * LANE] = jnp.where(keep, prev, nxt).astype(o_ref.dtype)
            prev = nxt


def _repack_in_proj(w_in, tr=512, tc=1536):
    depth, d, _ = w_in.shape
    per_tile = tc // LANE
    return pl.pallas_call(
        functools.partial(_repack_kernel, aligned_tiles=FLOGIT_OFF // tc, shift=RET_OFF - FLOGIT_OFF),
        grid=(depth, d // tr, MAIN_WIDTH // tc),
        in_specs=[
            pl.BlockSpec((None, tr, tc), lambda l, r, j: (l, r, j)),
            pl.BlockSpec((None, tr, LANE), lambda l, r, j: (l, r, (j + 1) * per_tile)),
        ],
        out_specs=pl.BlockSpec((None, tr, tc), lambda l, r, j: (l, r, j)),
        out_shape=jax.ShapeDtypeStruct((depth, d, MAIN_WIDTH), BF16),
        compiler_params=_params("parallel", "parallel", "parallel"),
        name="repack_in_proj",
    )(w_in, w_in)


def _flogit_weight_kernel(a_ref, o_ref, *, count):
    lane = lax.broadcasted_iota(jnp.int32, (1, LANE), 1)
    o_ref[...] = jnp.where(lane < count, a_ref[...], 0.0).astype(o_ref.dtype)


def _flogit_weights(w_in, tr=1024):
    depth, d, _ = w_in.shape
    return pl.pallas_call(
        functools.partial(_flogit_weight_kernel, count=FOX_HEADS),
        grid=(depth, d // tr),
        in_specs=[pl.BlockSpec((None, tr, LANE), lambda l, r: (l, r, FLOGIT_OFF // LANE))],
        out_specs=pl.BlockSpec((None, tr, LANE), lambda l, r: (l, r, 0)),
        out_shape=jax.ShapeDtypeStruct((depth, d, LANE), BF16),
        compiler_params=_params("parallel", "parallel"),
        name="forget_logit_weights",
    )(w_in)


def _mm_kernel(a_ref, w_ref, o_ref, *scratch, nk, relu2):
    part = jnp.dot(a_ref[...], w_ref[...], preferred_element_type=F32)

    def finish(acc):
        if relu2:
            acc = jnp.square(jnp.maximum(acc, 0.0))
        o_ref[...] = acc.astype(o_ref.dtype)

    if nk == 1:
        finish(part)
        return
    acc_ref, = scratch
    k = pl.program_id(2)

    @pl.when(k == 0)
    def _():
        acc_ref[...] = part

    @pl.when(jnp.logical_and(k > 0, k < nk - 1))
    def _():
        acc_ref[...] += part

    @pl.when(k == nk - 1)
    def _():
        finish(acc_ref[...] + part)


def _matmul(a, w, *, tm, tn, tk, out_dtype, layer=None, relu2=False, name="matmul"):
    m, kdim = a.shape
    n = w.shape[-1]
    nk = kdim // tk
    scratch = [] if nk == 1 else [pltpu.VMEM((tm, tn), F32)]
    if layer is None:
        w_spec = pl.BlockSpec((tk, tn), lambda i, j, k: (k, j))
    else:
        w_spec = pl.BlockSpec((None, tk, tn), lambda i, j, k: (layer, k, j))
    return pl.pallas_call(
        functools.partial(_mm_kernel, nk=nk, relu2=relu2),
        grid=(m // tm, n // tn, nk),
        in_specs=[
            pl.BlockSpec((tm, tk), lambda i, j, k: (i, k)),
            w_spec,
        ],
        out_specs=pl.BlockSpec((tm, tn), lambda i, j, k: (i, j)),
        out_shape=jax.ShapeDtypeStruct((m, n), out_dtype),
        scratch_shapes=scratch,
        compiler_params=_params("parallel", "parallel", "arbitrary"),
        name=name,
    )(a, w)


def _out_proj_kernel(yc_ref, yf_ref, yr_ref, w_ref, o_ref):
    f0 = yc_ref.shape[1]
    r0 = f0 + yf_ref.shape[1]
    acc = jnp.dot(yc_ref[...], w_ref[0:f0, :], preferred_element_type=F32)
    acc += jnp.dot(yf_ref[...], w_ref[f0:r0, :], preferred_element_type=F32)
    acc += jnp.dot(yr_ref[...], w_ref[r0:, :], preferred_element_type=F32)
    o_ref[...] = acc


def _out_proj(y_conv, y_fox, y_ret, w, tm=1024, tn=1024):
    m = y_conv.shape[0]
    kdim, n = w.shape

    def lhs(width):
        return pl.BlockSpec((tm, width), lambda i, j: (i, 0))

    return pl.pallas_call(
        _out_proj_kernel,
        grid=(m // tm, n // tn),
        in_specs=[lhs(y_conv.shape[1]), lhs(y_fox.shape[1]), lhs(y_ret.shape[1]),
                  pl.BlockSpec((kdim, tn), lambda i, j: (0, j))],
        out_specs=pl.BlockSpec((tm, tn), lambda i, j: (i, j)),
        out_shape=jax.ShapeDtypeStruct((m, n), F32),
        compiler_params=_params("parallel", "parallel"),
        name="out_proj",
    )(y_conv, y_fox, y_ret, w)


def _conv_kernel(u_ref, b_ref, c_ref, w_ref, o_ref, *, rows):
    seq, tc = u_ref.shape
    w0 = w_ref[0:1, :]
    w1 = w_ref[1:2, :]
    w2 = w_ref[2:3, :]
    rid = lax.broadcasted_iota(jnp.int32, (rows, 1), 0)

    def chunk(ci, carry):
        prev2, prev1 = carry
        r0 = pl.multiple_of(ci * rows, rows)
        z = c_ref[pl.ds(r0, rows), :].astype(F32) * u_ref[pl.ds(r0, rows), :].astype(F32)
        z1 = jnp.where(rid == 0, prev1, pltpu.roll(z, 1, 0))
        z2 = jnp.where(rid == 0, prev2, jnp.where(rid == 1, prev1, pltpu.roll(z, 2, 0)))
        y = w0 * z2 + w1 * z1 + w2 * z
        o_ref[pl.ds(r0, rows), :] = (b_ref[pl.ds(r0, rows), :].astype(F32) * y).astype(o_ref.dtype)
        return z[rows - 2:rows - 1, :], z[rows - 1:rows, :]

    zero = jnp.zeros((1, tc), F32)
    lax.fori_loop(0, seq // rows, chunk, (zero, zero))


def _short_conv(proj, conv_w, batch, seq, tc=256, rows=512):
    nblk = CONV_WIDTH // tc
    return pl.pallas_call(
        functools.partial(_conv_kernel, rows=rows),
        grid=(batch, nblk),
        in_specs=[
            pl.BlockSpec((seq, tc), lambda b, j: (b, j)),
            pl.BlockSpec((seq, tc), lambda b, j: (b, nblk + j)),
            pl.BlockSpec((seq, tc), lambda b, j: (b, 2 * nblk + j)),
            pl.BlockSpec((CONV_K, tc), lambda b, j: (0, j)),
        ],
        out_specs=pl.BlockSpec((seq, tc), lambda b, j: (b, j)),
        out_shape=jax.ShapeDtypeStruct((batch * seq, CONV_WIDTH), BF16),
        compiler_params=_params("parallel", "parallel"),
        name="short_conv",
    )(proj, proj, proj, conv_w)


def _cum_kernel(f_ref, bf_ref, o_ref, carry_ref):
    rows = f_ref.shape[0]

    @pl.when(pl.program_id(1) == 0)
    def _():
        carry_ref[...] = jnp.zeros_like(carry_ref)

    z = f_ref[...] + bf_ref[...]
    logf = jnp.minimum(z, 0.0) - jnp.log1p(jnp.exp(-jnp.abs(z)))
    p1 = logf.astype(BF16)
    r1 = logf - p1.astype(F32)
    p2 = r1.astype(BF16)
    p3 = (r1 - p2.astype(F32)).astype(BF16)
    ri = lax.broadcasted_iota(jnp.int32, (rows, rows), 0)
    ci = lax.broadcasted_iota(jnp.int32, (rows, rows), 1)
    tri = (ri >= ci).astype(BF16)
    cum = (jnp.dot(tri, p1, preferred_element_type=F32)
           + jnp.dot(tri, p2, preferred_element_type=F32)
           + jnp.dot(tri, p3, preferred_element_type=F32)) + carry_ref[...]
    o_ref[...] = cum.T
    carry_ref[...] = cum[rows - 1:rows, :]


def _forget_cumsum(f_logit, b_f_pad, batch, seq, rows):
    per_b = seq // rows
    return pl.pallas_call(
        _cum_kernel,
        grid=(batch, per_b),
        in_specs=[
            pl.BlockSpec((rows, LANE), lambda b, j: (b * per_b + j, 0)),
            pl.BlockSpec((1, LANE), lambda b, j: (0, 0)),
        ],
        out_specs=pl.BlockSpec((None, None, LANE, rows), lambda b, j: (b, j, 0, 0)),
        out_shape=jax.ShapeDtypeStruct((batch, per_b, LANE, rows), F32),
        scratch_shapes=[pltpu.VMEM((1, LANE), F32)],
        compiler_params=_params("parallel", "arbitrary"),
        name="forget_cumsum",
    )(f_logit, b_f_pad)


def _fox_kernel(q_ref, k_ref, v_ref, f_ref, o_ref, m_ref, l_ref, acc_ref, *, blk, hp):
    seq = q_ref.shape[0]
    inv_scale = HEAD_DIM ** 0.5
    exp2_scale = HEAD_DIM ** -0.5 * LOG2_E
    head0 = pl.program_id(1) * hp
    ri = lax.broadcasted_iota(jnp.int32, (blk, blk), 0)
    ci = lax.broadcasted_iota(jnp.int32, (blk, blk), 1)
    causal = ci <= ri

    def q_block(qi, _):
        q0 = pl.multiple_of(qi * blk, blk)
        m_ref[...] = jnp.full_like(m_ref, -jnp.inf)
        l_ref[...] = jnp.zeros_like(l_ref)
        acc_ref[...] = jnp.zeros_like(acc_ref)

        def kv_block(ki, masked):
            k0 = pl.multiple_of(ki * blk, blk)
            for hh in range(hp):
                cols = slice(hh * HEAD_DIM, (hh + 1) * HEAD_DIM)
                q = q_ref[pl.ds(q0, blk), cols]
                k = k_ref[pl.ds(k0, blk), cols]
                s = lax.dot_general(q, k, (((1,), (1,)), ((), ())), preferred_element_type=F32)
                u = s - f_ref[ki, pl.ds(head0 + hh, 1), :] * inv_scale
                if masked:
                    u = jnp.where(causal, u, -jnp.inf)
                m_old = m_ref[hh]
                m_new = jnp.maximum(m_old, jnp.max(u, axis=1, keepdims=True))
                p = jnp.exp2((u - m_new) * exp2_scale)
                alpha = jnp.exp2((m_old - m_new) * exp2_scale)
                l_ref[hh] = alpha * l_ref[hh] + jnp.sum(p, axis=1, keepdims=True)
                acc_ref[hh] = alpha * acc_ref[hh] + jnp.dot(
                    p.astype(BF16), v_ref[pl.ds(k0, blk), cols], preferred_element_type=F32)
                m_ref[hh] = m_new

        def full_block(ki, carry):
            kv_block(ki, False)
            return carry

        lax.fori_loop(0, qi, full_block, 0)
        kv_block(qi, True)
        for hh in range(hp):
            o_ref[pl.ds(q0, blk), hh * HEAD_DIM:(hh + 1) * HEAD_DIM] = (
                acc_ref[hh] / l_ref[hh]).astype(o_ref.dtype)
        return 0

    lax.fori_loop(0, seq // blk, q_block, 0)


def _forgetting_attention(proj, cum, batch, seq, blk, hp=2):
    groups = FOX_HEADS // hp
    width = hp * HEAD_DIM
    qoff = FOX_OFF // width
    fox_blocks = FOX_WIDTH // width
    head_rows = 16

    def head_spec(off):
        return pl.BlockSpec((seq, width), lambda b, g: (b, off + g))

    return pl.pallas_call(
        functools.partial(_fox_kernel, blk=blk, hp=hp),
        grid=(batch, groups),
        in_specs=[
            head_spec(qoff), head_spec(qoff + fox_blocks), head_spec(qoff + 2 * fox_blocks),
            pl.BlockSpec((None, seq // blk, head_rows, blk), lambda b, g: (b, 0, 0, 0)),
        ],
        out_specs=pl.BlockSpec((seq, width), lambda b, g: (b, g)),
        out_shape=jax.ShapeDtypeStruct((batch * seq, FOX_WIDTH), BF16),
        scratch_shapes=[
            pltpu.VMEM((hp, blk, 1), F32), pltpu.VMEM((hp, blk, 1), F32),
            pltpu.VMEM((hp, blk, HEAD_DIM), F32),
        ],
        compiler_params=_params("parallel", "parallel"),
        name="forgetting_attention",
    )(proj, proj, proj, cum)


def _ret_kernel(lg_ref, q_ref, k_ref, v_ref, g_ref, cos_ref, sin_ref, o_ref, *, chunk, hp):
    seq = q_ref.shape[0]
    scale = HEAD_DIM ** -0.5
    half = HEAD_DIM // 2
    ri = lax.broadcasted_iota(jnp.int32, (chunk, chunk), 0)
    ci = lax.broadcasted_iota(jnp.int32, (chunk, chunk), 1)
    diff = (ri - ci).astype(F32)
    jv = lax.broadcasted_iota(jnp.int32, (chunk, 1), 0).astype(F32)
    decays = []
    for hh in range(hp):
        lg = lg_ref[pl.program_id(1) * hp + hh]
        decays.append((
            jnp.where(diff >= 0.0, jnp.exp(jnp.maximum(diff, 0.0) * lg), 0.0),
            jnp.exp((jv + 1.0) * lg),
            jnp.exp((chunk - 1.0 - jv) * lg),
            jnp.exp(jnp.full((1, 1), float(chunk), F32) * lg),
        ))

    def rotate(t, cosf, sinf):
        return t * cosf + pltpu.roll(t, half, 1) * sinf

    def step(c, states):
        r0 = pl.multiple_of(c * chunk, chunk)
        rows = pl.ds(r0, chunk)
        cosf = cos_ref[rows, :]
        sinf = sin_ref[rows, :]
        new_states = []
        for hh in range(hp):
            cols = slice(hh * HEAD_DIM, (hh + 1) * HEAD_DIM)
            intra, q_decay, k_decay, chunk_decay = decays[hh]
            state = states[hh]
            q = rotate(q_ref[rows, cols].astype(F32), cosf, sinf)
            k = rotate(k_ref[rows, cols].astype(F32), cosf, sinf) * scale
            v = v_ref[rows, cols]
            scores = lax.dot_general(q.astype(BF16), k.astype(BF16), (((1,), (1,)), ((), ())),
                                     preferred_element_type=F32) * intra
            out = (jnp.dot(scores.astype(BF16), v, preferred_element_type=F32)
                   + jnp.dot((q * q_decay).astype(BF16), state.astype(BF16), preferred_element_type=F32))
            new_states.append(state * chunk_decay + lax.dot_general(
                (k * k_decay).astype(BF16), v, (((0,), (0,)), ((), ())), preferred_element_type=F32))
            mu = jnp.mean(out, axis=1, keepdims=True)
            cen = out - mu
            var = jnp.mean(cen * cen, axis=1, keepdims=True)
            y = cen * lax.rsqrt(var + GN_EPS)
            g = g_ref[rows, cols].astype(F32)
            o_ref[rows, cols] = (y * (g / (1.0 + jnp.exp(-g)))).astype(o_ref.dtype)
        return tuple(new_states)

    zero = jnp.zeros((HEAD_DIM, HEAD_DIM), F32)
    lax.fori_loop(0, seq // chunk, step, (zero,) * hp)


def _retention(proj, log_gamma, cosf, sinf, batch, seq, chunk=256, hp=2):
    groups = RET_HEADS // hp
    width = hp * HEAD_DIM
    qoff = RET_MAIN_OFF // width
    ret_blocks = RET_WIDTH // width

    def head_spec(off):
        return pl.BlockSpec((seq, width), lambda b, g, lg: (b, off + g))

    table = pl.BlockSpec((seq, HEAD_DIM), lambda b, g, lg: (0, 0))
    return pl.pallas_call(
        functools.partial(_ret_kernel, chunk=chunk, hp=hp),
        grid_spec=pltpu.PrefetchScalarGridSpec(
            num_scalar_prefetch=1,
            grid=(batch, groups),
            in_specs=[head_spec(qoff), head_spec(qoff + ret_blocks), head_spec(qoff + 2 * ret_blocks),
                      head_spec(qoff + 3 * ret_blocks), table, table],
            out_specs=pl.BlockSpec((seq, width), lambda b, g, lg: (b, g)),
        ),
        out_shape=jax.ShapeDtypeStruct((batch * seq, RET_WIDTH), BF16),
        compiler_params=_params("parallel", "parallel"),
        name="retention",
    )(log_gamma, proj, proj, proj, proj, cosf, sinf)


def _ln_kernel(x_ref, y_ref, g_ref, gain_ref, bias_ref, *rest, alpha, modulate):
    r = alpha * x_ref[...] + g_ref[...] * y_ref[...]
    mu = jnp.mean(r, axis=1, keepdims=True)
    cen = r - mu
    var = jnp.mean(cen * cen, axis=1, keepdims=True)
    xn = cen * lax.rsqrt(var + LN_EPS) * gain_ref[...] + bias_ref[...]
    if modulate:
        sc_ref, sh_ref, xo_ref, h_ref = rest
        xo_ref[...] = xn
        h_ref[...] = (xn * (1.0 + sc_ref[...]) + sh_ref[...]).astype(h_ref.dtype)
    else:
        xo_ref, = rest
        xo_ref[...] = xn


def _residual_ln(x2d, branch, gate, gain, bias, sc, sh, seq, alpha, tm=256):
    m, d = x2d.shape
    per_b = seq // tm
    modulate = sc is not None
    row = pl.BlockSpec((tm, d), lambda i: (i, 0))
    vec_b = pl.BlockSpec((None, 1, d), lambda i: (i // per_b, 0, 0))
    vec = pl.BlockSpec((1, d), lambda i: (0, 0))
    in_specs = [row, row, vec_b, vec, vec]
    args = [x2d, branch, gate, gain, bias]
    out_specs = [row]
    out_shape = [jax.ShapeDtypeStruct((m, d), F32)]
    if modulate:
        in_specs += [vec_b, vec_b]
        args += [sc, sh]
        out_specs.append(row)
        out_shape.append(jax.ShapeDtypeStruct((m, d), BF16))
    res = pl.pallas_call(
        functools.partial(_ln_kernel, alpha=alpha, modulate=modulate),
        grid=(m // tm,),
        in_specs=in_specs,
        out_specs=out_specs,
        out_shape=out_shape,
        compiler_params=_params("parallel"),
        name="residual_layernorm",
    )(*args)
    return (res[0], res[1]) if modulate else (res[0], None)


def _rotary_tables(seq):
    pos = np.arange(seq, dtype=np.float32)
    inv_freq = (np.float32(ROPE_BASE) ** (-np.arange(0, HEAD_DIM, 2, dtype=np.float32) / np.float32(HEAD_DIM))
                ).astype(np.float32)
    ang = (pos[:, None] * inv_freq[None, :]).astype(np.float32)
    cos = np.cos(ang).astype(np.float32)
    sin = np.sin(ang).astype(np.float32)
    return (jnp.asarray(np.concatenate([cos, cos], axis=1)),
            jnp.asarray(np.concatenate([-sin, sin], axis=1)))


def _log_gamma():
    h = np.arange(RET_HEADS, dtype=np.float32)
    return jnp.asarray(np.log1p(-np.exp2(np.float32(-5.0) - h)).astype(np.float32))


def kernel(x, c, w_ada, b_ada, w_in, b_f, conv_w, w_out, ln1_g, ln1_b, w_up, w_down, ln2_g, ln2_b):
    batch, seq, d = x.shape
    depth = w_ada.shape[0]
    m = batch * seq
    alpha = (2.0 * depth) ** 0.25
    pad_rows = 8

    c_pad = jnp.zeros((pad_rows, d), F32).at[:batch].set(c)
    mod = _ada(c_pad, w_ada, b_ada)
    mod = mod[:, :batch].reshape(depth, batch, 6, 1, d)

    cosf, sinf = _rotary_tables(seq)
    log_gamma = _log_gamma()

    w_main = _repack_in_proj(w_in)
    w_f = _flogit_weights(w_in)
    b_f_pad = jnp.zeros((depth, 1, LANE), F32).at[:, 0, :FOX_HEADS].set(b_f)
    fox_blk = 512

    x2d = x.reshape(m, d)
    h = _modulate(x2d, mod[0, :, 1], mod[0, :, 0], seq)
    for l in range(depth):
        sh_m, sc_m, g_a, g_m = mod[l, :, 3], mod[l, :, 4], mod[l, :, 2], mod[l, :, 5]

        proj = _matmul(h, w_main, layer=l, tm=1024, tn=1536, tk=d, out_dtype=BF16, name="in_proj")
        f_logit = _matmul(h, w_f, layer=l, tm=1024, tn=LANE, tk=d, out_dtype=F32, name="forget_logits")

        cum = _forget_cumsum(f_logit, b_f_pad[l], batch, seq, rows=fox_blk)
        y_conv = _short_conv(proj, conv_w[l], batch, seq)
        y_fox = _forgetting_attention(proj, cum, batch, seq, blk=fox_blk)
        y_ret = _retention(proj, log_gamma, cosf, sinf, batch, seq)

        mix = _out_proj(y_conv, y_fox, y_ret, w_out[l].astype(BF16))
        x2d, h2 = _residual_ln(x2d, mix, g_a, ln1_g[l][None], ln1_b[l][None], sc_m, sh_m, seq, alpha)

        a = _matmul(h2, w_up[l].astype(BF16), tm=1024, tn=1024, tk=d, out_dtype=BF16, relu2=True,
                    name="mlp_up")
        mlp = _matmul(a, w_down[l].astype(BF16), tm=1024, tn=1024, tk=d, out_dtype=F32, name="mlp_down")
        if l + 1 < depth:
            x2d, h = _residual_ln(x2d, mlp, g_m, ln2_g[l][None], ln2_b[l][None],
                                  mod[l + 1, :, 1], mod[l + 1, :, 0], seq, alpha)
        else:
            x2d, _ = _residual_ln(x2d, mlp, g_m, ln2_g[l][None], ln2_b[l][None], None, None, seq, alpha)
    return x2d.reshape(batch, seq, d)
```

```python
import functools

import numpy as np
import jax
import jax.numpy as jnp
from jax import lax
from jax.experimental import pallas as pl
from jax.experimental.pallas import tpu as pltpu

D_MODEL = 4096
HEAD_DIM = 128
CONV_WIDTH = D_MODEL // 4
FOX_WIDTH = (D_MODEL - CONV_WIDTH) // 2
RET_WIDTH = D_MODEL - CONV_WIDTH - FOX_WIDTH
FOX_HEADS = FOX_WIDTH // HEAD_DIM
RET_HEADS = RET_WIDTH // HEAD_DIM
CONV_K = 3
ROPE_BASE = 10000.0
LN_EPS = 1e-5
GN_EPS = 1e-6

FOX_OFF = 3 * CONV_WIDTH
FLOGIT_OFF = FOX_OFF + 3 * FOX_WIDTH
RET_OFF = FLOGIT_OFF + FOX_HEADS
MAIN_WIDTH = 3 * CONV_WIDTH + 3 * FOX_WIDTH + 4 * RET_WIDTH
RET_MAIN_OFF = FOX_OFF + 3 * FOX_WIDTH

LANE = 128
LOG2_E = 1.4426950408889634

FOX_TQ = 1024
FOX_TK = 512
FOX_HP = 2
VMEM_LIMIT = 56 * 1024 * 1024

BF16 = jnp.bfloat16
F32 = jnp.float32


def _params(*sem):
    return pltpu.CompilerParams(dimension_semantics=sem, vmem_limit_bytes=VMEM_LIMIT)


def _ada_kernel(c_ref, w_ref, b_ref, o_ref):
    c = c_ref[...]
    cond = (c / (1.0 + jnp.exp(-c))).astype(BF16)
    acc = jnp.dot(cond, w_ref[...].astype(BF16), preferred_element_type=F32)
    o_ref[...] = acc + b_ref[...]


def _ada(c_pad, w_ada, b_ada, tn=512):
    depth, d, n = w_ada.shape
    rows = c_pad.shape[0]
    return pl.pallas_call(
        _ada_kernel,
        grid=(depth, n // tn),
        in_specs=[
            pl.BlockSpec((rows, d), lambda l, j: (0, 0)),
            pl.BlockSpec((None, d, tn), lambda l, j: (l, 0, j)),
            pl.BlockSpec((None, 1, tn), lambda l, j: (l, 0, j)),
        ],
        out_specs=pl.BlockSpec((None, rows, tn), lambda l, j: (l, 0, j)),
        out_shape=jax.ShapeDtypeStruct((depth, rows, n), F32),
        compiler_params=_params("parallel", "parallel"),
        name="ada_modulation",
    )(c_pad, w_ada, b_ada.reshape(depth, 1, n))


def _modulate_kernel(x_ref, sc_ref, sh_ref, h_ref):
    h_ref[...] = (x_ref[...] * (1.0 + sc_ref[...]) + sh_ref[...]).astype(h_ref.dtype)


def _modulate(x2d, sc, sh, seq, tm=512):
    m, d = x2d.shape
    per_b = seq // tm
    vec = pl.BlockSpec((None, 1, d), lambda i: (i // per_b, 0, 0))
    return pl.pallas_call(
        _modulate_kernel,
        grid=(m // tm,),
        in_specs=[pl.BlockSpec((tm, d), lambda i: (i, 0)), vec, vec],
        out_specs=pl.BlockSpec((tm, d), lambda i: (i, 0)),
        out_shape=jax.ShapeDtypeStruct((m, d), BF16),
        compiler_params=_params("parallel"),
        name="modulate",
    )(x2d, sc, sh)


def _repack_kernel(a_ref, b_ref, o_ref, *, aligned_tiles, shift):
    j = pl.program_id(2)
    ntile = o_ref.shape[1] // LANE

    @pl.when(j < aligned_tiles)
    def _():
        o_ref[...] = a_ref[...].astype(o_ref.dtype)

    @pl.when(j >= aligned_tiles)
    def _():
        keep = lax.broadcasted_iota(jnp.int32, (1, LANE), 1) < (LANE - shift)
        prev = pltpu.roll(a_ref[:, 0:LANE], LANE - shift, 1)
        for t in range(ntile):
            src = a_ref[:, (t + 1) * LANE:(t + 2) * LANE] if t + 1 < ntile else b_ref[...]
            nxt = pltpu.roll(src, LANE - shift, 1)
            o_ref[:, t * LANE:(t + 1) * LANE] = jnp.where(keep, prev, nxt).astype(o_ref.dtype)
            prev = nxt


def _repack_in_proj(w_in, tr=512, tc=1536):
    depth, d, _ = w_in.shape
    per_tile = tc // LANE
    return pl.pallas_call(
        functools.partial(_repack_kernel, aligned_tiles=FLOGIT_OFF // tc, shift=RET_OFF - FLOGIT_OFF),
        grid=(depth, d // tr, MAIN_WIDTH // tc),
        in_specs=[
            pl.BlockSpec((None, tr, tc), lambda l, r, j: (l, r, j)),
            pl.BlockSpec((None, tr, LANE), lambda l, r, j: (l, r, (j + 1) * per_tile)),
        ],
        out_specs=pl.BlockSpec((None, tr, tc), lambda l, r, j: (l, r, j)),
        out_shape=jax.ShapeDtypeStruct((depth, d, MAIN_WIDTH), BF16),
        compiler_params=_params("parallel", "parallel", "parallel"),
        name="repack_in_proj",
    )(w_in, w_in)


def _flogit_weight_kernel(a_ref, o_ref, *, count):
    lane = lax.broadcasted_iota(jnp.int32, (1, LANE), 1)
    o_ref[...] = jnp.where(lane < count, a_ref[...], 0.0).astype(o_ref.dtype)


def _flogit_weights(w_in, tr=1024):
    depth, d, _ = w_in.shape
    return pl.pallas_call(
        functools.partial(_flogit_weight_kernel, count=FOX_HEADS),
        grid=(depth, d // tr),
        in_specs=[pl.BlockSpec((None, tr, LANE), lambda l, r: (l, r, FLOGIT_OFF // LANE))],
        out_specs=pl.BlockSpec((None, tr, LANE), lambda l, r: (l, r, 0)),
        out_shape=jax.ShapeDtypeStruct((depth, d, LANE), BF16),
        compiler_params=_params("parallel", "parallel"),
        name="forget_logit_weights",
    )(w_in)


def _mm_kernel(a_ref, w_ref, o_ref, *scratch, nk, relu2):
    part = jnp.dot(a_ref[...], w_ref[...], preferred_element_type=F32)

    def finish(acc):
        if relu2:
            acc = jnp.square(jnp.maximum(acc, 0.0))
        o_ref[...] = acc.astype(o_ref.dtype)

    if nk == 1:
        finish(part)
        return
    acc_ref, = scratch
    k = pl.program_id(2)

    @pl.when(k == 0)
    def _():
        acc_ref[...] = part

    @pl.when(jnp.logical_and(k > 0, k < nk - 1))
    def _():
        acc_ref[...] += part

    @pl.when(k == nk - 1)
    def _():
        finish(acc_ref[...] + part)


def _matmul(a, w, *, tm, tn, tk, out_dtype, layer=None, relu2=False, name="matmul"):
    m, kdim = a.shape
    n = w.shape[-1]
    nk = kdim // tk
    scratch = [] if nk == 1 else [pltpu.VMEM((tm, tn), F32)]
    if layer is None:
        w_spec = pl.BlockSpec((tk, tn), lambda i, j, k: (k, j))
    else:
        w_spec = pl.BlockSpec((None, tk, tn), lambda i, j, k: (layer, k, j))
    return pl.pallas_call(
        functools.partial(_mm_kernel, nk=nk, relu2=relu2),
        grid=(m // tm, n // tn, nk),
        in_specs=[
            pl.BlockSpec((tm, tk), lambda i, j, k: (i, k)),
            w_spec,
        ],
        out_specs=pl.BlockSpec((tm, tn), lambda i, j, k: (i, j)),
        out_shape=jax.ShapeDtypeStruct((m, n), out_dtype),
        scratch_shapes=scratch,
        compiler_params=_params("parallel", "parallel", "arbitrary"),
        name=name,
    )(a, w)


def _cast_weight_block(w_ref, wb_ref, rows=512):
    def body(r, carry):
        r0 = pl.multiple_of(r * rows, rows)
        wb_ref[pl.ds(r0, rows), :] = w_ref[pl.ds(r0, rows), :].astype(wb_ref.dtype)
        return carry

    lax.fori_loop(0, w_ref.shape[0] // rows, body, 0)


def _mm_wcast_kernel(a_ref, w_ref, o_ref, wb_ref, *, relu2):
    @pl.when(pl.program_id(1) == 0)
    def _():
        _cast_weight_block(w_ref, wb_ref)

    acc = jnp.dot(a_ref[...], wb_ref[...], preferred_element_type=F32)
    if relu2:
        acc = jnp.square(jnp.maximum(acc, 0.0))
    o_ref[...] = acc.astype(o_ref.dtype)


def _matmul_wcast(a, w, layer, *, tm, tn, out_dtype, relu2=False, name="matmul_wcast"):
    m, kdim = a.shape
    n = w.shape[-1]
    return pl.pallas_call(
        functools.partial(_mm_wcast_kernel, relu2=relu2),
        grid=(n // tn, m // tm),
        in_specs=[
            pl.BlockSpec((tm, kdim), lambda j, i: (i, 0)),
            pl.BlockSpec((None, kdim, tn), lambda j, i: (layer, 0, j)),
        ],
        out_specs=pl.BlockSpec((tm, tn), lambda j, i: (i, j)),
        out_shape=jax.ShapeDtypeStruct((m, n), out_dtype),
        scratch_shapes=[pltpu.VMEM((kdim, tn), BF16)],
        compiler_params=_params("parallel", "arbitrary"),
        name=name,
    )(a, w)


def _out_proj_kernel(yc_ref, yf_ref, yr_ref, w_ref, o_ref, wb_ref):
    @pl.when(pl.program_id(1) == 0)
    def _():
        _cast_weight_block(w_ref, wb_ref)

    f0 = yc_ref.shape[1]
    r0 = f0 + yf_ref.shape[1]
    acc = jnp.dot(yc_ref[...], wb_ref[0:f0, :], preferred_element_type=F32)
    acc += jnp.dot(yf_ref[...], wb_ref[f0:r0, :], preferred_element_type=F32)
    acc += jnp.dot(yr_ref[...], wb_ref[r0:, :], preferred_element_type=F32)
    o_ref[...] = acc


def _out_proj(y_conv, y_fox, y_ret, w, layer, tm=1024, tn=512):
    m = y_conv.shape[0]
    _, kdim, n = w.shape

    def lhs(width):
        return pl.BlockSpec((tm, width), lambda j, i: (i, 0))

    return pl.pallas_call(
        _out_proj_kernel,
        grid=(n // tn, m // tm),
        in_specs=[lhs(y_conv.shape[1]), lhs(y_fox.shape[1]), lhs(y_ret.shape[1]),
                  pl.BlockSpec((None, kdim, tn), lambda j, i: (layer, 0, j))],
        out_specs=pl.BlockSpec((tm, tn), lambda j, i: (i, j)),
        out_shape=jax.ShapeDtypeStruct((m, n), F32),
        scratch_shapes=[pltpu.VMEM((kdim, tn), BF16)],
        compiler_params=_params("parallel", "arbitrary"),
        name="out_proj",
    )(y_conv, y_fox, y_ret, w)


def _conv_kernel(u_ref, b_ref, c_ref, w_ref, o_ref, *, rows):
    seq, tc = u_ref.shape
    w0 = w_ref[0:1, :]
    w1 = w_ref[1:2, :]
    w2 = w_ref[2:3, :]
    rid = lax.broadcasted_iota(jnp.int32, (rows, 1), 0)

    def chunk(ci, carry):
        prev2, prev1 = carry
        r0 = pl.multiple_of(ci * rows, rows)
        z = c_ref[pl.ds(r0, rows), :].astype(F32) * u_ref[pl.ds(r0, rows), :].astype(F32)
        z1 = jnp.where(rid == 0, prev1, pltpu.roll(z, 1, 0))
        z2 = jnp.where(rid == 0, prev2, jnp.where(rid == 1, prev1, pltpu.roll(z, 2, 0)))
        y = w0 * z2 + w1 * z1 + w2 * z
        o_ref[pl.ds(r0, rows), :] = (b_ref[pl.ds(r0, rows), :].astype(F32) * y).astype(o_ref.dtype)
        return z[rows - 2:rows - 1, :], z[rows - 1:rows, :]

    zero = jnp.zeros((1, tc), F32)
    lax.fori_loop(0, seq // rows, chunk, (zero, zero))


def _short_conv(proj, conv_w, batch, seq, tc=256, rows=512):
    nblk = CONV_WIDTH // tc
    return pl.pallas_call(
        functools.partial(_conv_kernel, rows=rows),
        grid=(batch, nblk),
        in_specs=[
            pl.BlockSpec((seq, tc), lambda b, j: (b, j)),
            pl.BlockSpec((seq, tc), lambda b, j: (b, nblk + j)),
            pl.BlockSpec((seq, tc), lambda b, j: (b, 2 * nblk + j)),
            pl.BlockSpec((CONV_K, tc), lambda b, j: (0, j)),
        ],
        out_specs=pl.BlockSpec((seq, tc), lambda b, j: (b, j)),
        out_shape=jax.ShapeDtypeStruct((batch * seq, CONV_WIDTH), BF16),
        compiler_params=_params("parallel", "parallel"),
        name="short_conv",
    )(proj, proj, proj, conv_w)


def _cum_kernel(f_ref, bf_ref, o_ref, carry_ref):
    rows = f_ref.shape[0]

    @pl.when(pl.program_id(1) == 0)
    def _():
        carry_ref[...] = jnp.zeros_like(carry_ref)

    z = f_ref[...] + bf_ref[...]
    logf = jnp.minimum(z, 0.0) - jnp.log1p(jnp.exp(-jnp.abs(z)))
    p1 = logf.astype(BF16)
    r1 = logf - p1.astype(F32)
    p2 = r1.astype(BF16)
    p3 = (r1 - p2.astype(F32)).astype(BF16)
    ri = lax.broadcasted_iota(jnp.int32, (rows, rows), 0)
    ci = lax.broadcasted_iota(jnp.int32, (rows, rows), 1)
    tri = (ri >= ci).astype(BF16)
    cum = (jnp.dot(tri, p1, preferred_element_type=F32)
           + jnp.dot(tri, p2, preferred_element_type=F32)
           + jnp.dot(tri, p3, preferred_element_type=F32)) + carry_ref[...]
    o_ref[...] = cum.T
    carry_ref[...] = cum[rows - 1:rows, :]


def _forget_cumsum(f_logit, b_f_pad, batch, seq, rows):
    per_b = seq // rows
    return pl.pallas_call(
        _cum_kernel,
        grid=(batch, per_b),
        in_specs=[
            pl.BlockSpec((rows, LANE), lambda b, j: (b * per_b + j, 0)),
            pl.BlockSpec((1, LANE), lambda b, j: (0, 0)),
        ],
        out_specs=pl.BlockSpec((None, None, LANE, rows), lambda b, j: (b, j, 0, 0)),
        out_shape=jax.ShapeDtypeStruct((batch, per_b, LANE, rows), F32),
        scratch_shapes=[pltpu.VMEM((1, LANE), F32)],
        compiler_params=_params("parallel", "arbitrary"),
        name="forget_cumsum",
    )(f_logit, b_f_pad)


def _fox_kernel(q_ref, k_ref, v_ref, f_ref, o_ref, vt_ref, frep_ref, m_ref, l_ref, acc_ref, *, tq, tk, hp):
    seq = q_ref.shape[0]
    inv_scale = HEAD_DIM ** 0.5
    exp2_scale = HEAD_DIM ** -0.5 * LOG2_E
    head0 = pl.program_id(1) * hp
    per_q = tq // tk
    rel_pos = (lax.broadcasted_iota(jnp.int32, (tk, tq), 1)
               - lax.broadcasted_iota(jnp.int32, (tk, tq), 0))

    def head_cols(hh):
        return slice(hh * HEAD_DIM, (hh + 1) * HEAD_DIM)

    for hh in range(hp):
        def prepare(j, carry, hh=hh):
            r0 = pl.multiple_of(j * tk, tk)
            vt_ref[hh, j] = v_ref[pl.ds(r0, tk), head_cols(hh)].astype(F32).T.astype(BF16)
            f_row = f_ref[j, pl.ds(head0 + hh, 1), :] * inv_scale
            frep_ref[hh, j] = jnp.broadcast_to(f_row, (LANE, tk)).T
            return carry

        lax.fori_loop(0, seq // tk, prepare, 0)

    def q_block(qi, _):
        q0 = pl.multiple_of(qi * tq, tq)
        m_ref[...] = jnp.full_like(m_ref, -jnp.inf)
        l_ref[...] = jnp.zeros_like(l_ref)
        acc_ref[...] = jnp.zeros_like(acc_ref)

        def kv_block(ki, diag):
            k0 = pl.multiple_of(ki * tk, tk)
            for hh in range(hp):
                q = q_ref[pl.ds(q0, tq), head_cols(hh)]
                k = k_ref[pl.ds(k0, tk), head_cols(hh)]
                st = lax.dot_general(k, q, (((1,), (1,)), ((), ())), preferred_element_type=F32)
                u = st - jnp.concatenate([frep_ref[hh, ki]] * (tq // LANE), axis=1)
                if diag is not None:
                    u = jnp.where(rel_pos >= diag * tk, u, -jnp.inf)
                m_old = m_ref[hh]
                m_new = jnp.maximum(m_old, jnp.max(u, axis=0, keepdims=True))
                p = jnp.exp2((u - m_new) * exp2_scale)
                alpha = jnp.exp2((m_old - m_new) * exp2_scale)
                l_ref[hh] = alpha * l_ref[hh] + jnp.sum(p, axis=0, keepdims=True)
                acc_ref[hh] = alpha * acc_ref[hh] + jnp.dot(
                    vt_ref[hh, ki], p.astype(BF16), preferred_element_type=F32)
                m_ref[hh] = m_new

        def full_block(ki, carry):
            kv_block(ki, None)
            return carry

        lax.fori_loop(0, qi * per_q, full_block, 0)
        for d in range(per_q):
            kv_block(qi * per_q + d, d)
        for hh in range(hp):
            o_ref[pl.ds(q0, tq), head_cols(hh)] = (acc_ref[hh] / l_ref[hh]).T.astype(o_ref.dtype)
        return 0

    lax.fori_loop(0, seq // tq, q_block, 0)


def _forgetting_attention(proj, cum, batch, seq, tq, tk, hp):
    groups = FOX_HEADS // hp
    width = hp * HEAD_DIM
    qoff = FOX_OFF // width
    fox_blocks = FOX_WIDTH // width
    head_rows = 16

    def head_spec(off):
        return pl.BlockSpec((seq, width), lambda b, g: (b, off + g))

    return pl.pallas_call(
        functools.partial(_fox_kernel, tq=tq, tk=tk, hp=hp),
        grid=(batch, groups),
        in_specs=[
            head_spec(qoff), head_spec(qoff + fox_blocks), head_spec(qoff + 2 * fox_blocks),
            pl.BlockSpec((None, seq // tk, head_rows, tk), lambda b, g: (b, 0, 0, 0)),
        ],
        out_specs=pl.BlockSpec((seq, width), lambda b, g: (b, g)),
        out_shape=jax.ShapeDtypeStruct((batch * seq, FOX_WIDTH), BF16),
        scratch_shapes=[
            pltpu.VMEM((hp, seq // tk, HEAD_DIM, tk), BF16),
            pltpu.VMEM((hp, seq // tk, tk, LANE), F32),
            pltpu.VMEM((hp, 1, tq), F32), pltpu.VMEM((hp, 1, tq), F32),
            pltpu.VMEM((hp, HEAD_DIM, tq), F32),
        ],
        compiler_params=_params("parallel", "parallel"),
        name="forgetting_attention",
    )(proj, proj, proj, cum)


def _ret_kernel(lg_ref, q_ref, k_ref, v_ref, g_ref, cos_ref, sin_ref, o_ref, *, chunk, hp):
    seq = q_ref.shape[0]
    scale = HEAD_DIM ** -0.5
    half = HEAD_DIM // 2
    ri = lax.broadcasted_iota(jnp.int32, (chunk, chunk), 0)
    ci = lax.broadcasted_iota(jnp.int32, (chunk, chunk), 1)
    diff = (ri - ci).astype(F32)
    jv = lax.broadcasted_iota(jnp.int32, (chunk, 1), 0).astype(F32)
    decays = []
    for hh in range(hp):
        lg = lg_ref[pl.program_id(1) * hp + hh]
        decays.append((
            jnp.where(diff >= 0.0, jnp.exp(jnp.maximum(diff, 0.0) * lg), 0.0),
            jnp.exp((jv + 1.0) * lg),
            jnp.exp((chunk - 1.0 - jv) * lg),
            jnp.exp(jnp.full((1, 1), float(chunk), F32) * lg),
        ))

    def rotate(t, cosf, sinf):
        return t * cosf + pltpu.roll(t, half, 1) * sinf

    def step(c, states):
        r0 = pl.multiple_of(c * chunk, chunk)
        rows = pl.ds(r0, chunk)
        cosf = cos_ref[rows, :]
        sinf = sin_ref[rows, :]
        new_states = []
        for hh in range(hp):
            cols = slice(hh * HEAD_DIM, (hh + 1) * HEAD_DIM)
            intra, q_decay, k_decay, chunk_decay = decays[hh]
            state = states[hh]
            q = rotate(q_ref[rows, cols].astype(F32), cosf, sinf)
            k = rotate(k_ref[rows, cols].astype(F32), cosf, sinf) * scale
            v = v_ref[rows, cols]
            scores = lax.dot_general(q.astype(BF16), k.astype(BF16), (((1,), (1,)), ((), ())),
                                     preferred_element_type=F32) * intra
            out = (jnp.dot(scores.astype(BF16), v, preferred_element_type=F32)
                   + jnp.dot((q * q_decay).astype(BF16), state.astype(BF16), preferred_element_type=F32))
            new_states.append(state * chunk_decay + lax.dot_general(
                (k * k_decay).astype(BF16), v, (((0,), (0,)), ((), ())), preferred_element_type=F32))
            mu = jnp.mean(out, axis=1, keepdims=True)
            cen = out - mu
            var = jnp.mean(cen * cen, axis=1, keepdims=True)
            y = cen * lax.rsqrt(var + GN_EPS)
            g = g_ref[rows, cols].astype(F32)
            o_ref[rows, cols] = (y * (g / (1.0 + jnp.exp(-g)))).astype(o_ref.dtype)
        return tuple(new_states)

    zero = jnp.zeros((HEAD_DIM, HEAD_DIM), F32)
    lax.fori_loop(0, seq // chunk, step, (zero,) * hp)


def _retention(proj, log_gamma, cosf, sinf, batch, seq, chunk=256, hp=2):
    groups = RET_HEADS // hp
    width = hp * HEAD_DIM
    qoff = RET_MAIN_OFF // width
    ret_blocks = RET_WIDTH // width

    def head_spec(off):
        return pl.BlockSpec((seq, width), lambda b, g, lg: (b, off + g))

    table = pl.BlockSpec((seq, HEAD_DIM), lambda b, g, lg: (0, 0))
    return pl.pallas_call(
        functools.partial(_ret_kernel, chunk=chunk, hp=hp),
        grid_spec=pltpu.PrefetchScalarGridSpec(
            num_scalar_prefetch=1,
            grid=(batch, groups),
            in_specs=[head_spec(qoff), head_spec(qoff + ret_blocks), head_spec(qoff + 2 * ret_blocks),
                      head_spec(qoff + 3 * ret_blocks), table, table],
            out_specs=pl.BlockSpec((seq, width), lambda b, g, lg: (b, g)),
        ),
        out_shape=jax.ShapeDtypeStruct((batch * seq, RET_WIDTH), BF16),
        compiler_params=_params("parallel", "parallel"),
        name="retention",
    )(log_gamma, proj, proj, proj, proj, cosf, sinf)


def _ln_kernel(x_ref, y_ref, g_ref, gain_ref, bias_ref, *rest, alpha, modulate):
    r = alpha * x_ref[...] + g_ref[...] * y_ref[...]
    mu = jnp.mean(r, axis=1, keepdims=True)
    cen = r - mu
    var = jnp.mean(cen * cen, axis=1, keepdims=True)
    xn = cen * lax.rsqrt(var + LN_EPS) * gain_ref[...] + bias_ref[...]
    if modulate:
        sc_ref, sh_ref, xo_ref, h_ref = rest
        xo_ref[...] = xn
        h_ref[...] = (xn * (1.0 + sc_ref[...]) + sh_ref[...]).astype(h_ref.dtype)
    else:
        xo_ref, = rest
        xo_ref[...] = xn


def _residual_ln(x2d, branch, gate, gain, bias, sc, sh, seq, alpha, tm=256):
    m, d = x2d.shape
    per_b = seq // tm
    modulate = sc is not None
    row = pl.BlockSpec((tm, d), lambda i: (i, 0))
    vec_b = pl.BlockSpec((None, 1, d), lambda i: (i // per_b, 0, 0))
    vec = pl.BlockSpec((1, d), lambda i: (0, 0))
    in_specs = [row, row, vec_b, vec, vec]
    args = [x2d, branch, gate, gain, bias]
    out_specs = [row]
    out_shape = [jax.ShapeDtypeStruct((m, d), F32)]
    if modulate:
        in_specs += [vec_b, vec_b]
        args += [sc, sh]
        out_specs.append(row)
        out_shape.append(jax.ShapeDtypeStruct((m, d), BF16))
    res = pl.pallas_call(
        functools.partial(_ln_kernel, alpha=alpha, modulate=modulate),
        grid=(m // tm,),
        in_specs=in_specs,
        out_specs=out_specs,
        out_shape=out_shape,
        compiler_params=_params("parallel"),
        name="residual_layernorm",
    )(*args)
    return (res[0], res[1]) if modulate else (res[0], None)


def _rotary_tables(seq):
    pos = np.arange(seq, dtype=np.float32)
    inv_freq = (np.float32(ROPE_BASE) ** (-np.arange(0, HEAD_DIM, 2, dtype=np.float32) / np.float32(HEAD_DIM))
                ).astype(np.float32)
    ang = (pos[:, None] * inv_freq[None, :]).astype(np.float32)
    cos = np.cos(ang).astype(np.float32)
    sin = np.sin(ang).astype(np.float32)
    return (jnp.asarray(np.concatenate([cos, cos], axis=1)),
            jnp.asarray(np.concatenate([-sin, sin], axis=1)))


def _log_gamma():
    h = np.arange(RET_HEADS, dtype=np.float32)
    return jnp.asarray(np.log1p(-np.exp2(np.float32(-5.0) - h)).astype(np.float32))


def kernel(x, c, w_ada, b_ada, w_in, b_f, conv_w, w_out, ln1_g, ln1_b, w_up, w_down, ln2_g, ln2_b):
    batch, seq, d = x.shape
    depth = w_ada.shape[0]
    m = batch * seq
    alpha = (2.0 * depth) ** 0.25
    pad_rows = 8

    c_pad = jnp.zeros((pad_rows, d), F32).at[:batch].set(c)
    mod = _ada(c_pad, w_ada, b_ada)
    mod = mod[:, :batch].reshape(depth, batch, 6, 1, d)

    cosf, sinf = _rotary_tables(seq)
    log_gamma = _log_gamma()

    w_main = _repack_in_proj(w_in)
    w_f = _flogit_weights(w_in)
    b_f_pad = jnp.zeros((depth, 1, LANE), F32).at[:, 0, :FOX_HEADS].set(b_f)
    w_down_bf16 = w_down.astype(BF16)

    x2d = x.reshape(m, d)
    h = _modulate(x2d, mod[0, :, 1], mod[0, :, 0], seq)
    for l in range(depth):
        sh_m, sc_m, g_a, g_m = mod[l, :, 3], mod[l, :, 4], mod[l, :, 2], mod[l, :, 5]

        proj = _matmul(h, w_main, layer=l, tm=1024, tn=1536, tk=d, out_dtype=BF16, name="in_proj")
        f_logit = _matmul(h, w_f, layer=l, tm=1024, tn=LANE, tk=d, out_dtype=F32, name="forget_logits")

        cum = _forget_cumsum(f_logit, b_f_pad[l], batch, seq, rows=FOX_TK)
        y_conv = _short_conv(proj, conv_w[l], batch, seq)
        y_fox = _forgetting_attention(proj, cum, batch, seq, tq=FOX_TQ, tk=FOX_TK, hp=FOX_HP)
        y_ret = _retention(proj, log_gamma, cosf, sinf, batch, seq)

        mix = _out_proj(y_conv, y_fox, y_ret, w_out, l)
        x2d, h2 = _residual_ln(x2d, mix, g_a, ln1_g[l][None], ln1_b[l][None], sc_m, sh_m, seq, alpha)

        a = _matmul_wcast(h2, w_up, l, tm=1024, tn=512, out_dtype=BF16, relu2=True, name="mlp_up")
        mlp = _matmul(a, w_down_bf16, layer=l, tm=1024, tn=1024, tk=d, out_dtype=F32, name="mlp_down")
        if l + 1 < depth:
            x2d, h = _residual_ln(x2d, mlp, g_m, ln2_g[l][None], ln2_b[l][None],
                                  mod[l + 1, :, 1], mod[l + 1, :, 0], seq, alpha)
        else:
            x2d, _ = _residual_ln(x2d, mlp, g_m, ln2_g[l][None], ln2_b[l][None], None, None, seq, alpha)
    return x2d.reshape(batch, seq, d)
```

```python
import functools

import numpy as np
import jax
import jax.numpy as jnp
from jax import lax
from jax.experimental import pallas as pl
from jax.experimental.pallas import tpu as pltpu

D_MODEL = 4096
HEAD_DIM = 128
CONV_WIDTH = D_MODEL // 4
FOX_WIDTH = (D_MODEL - CONV_WIDTH) // 2
RET_WIDTH = D_MODEL - CONV_WIDTH - FOX_WIDTH
FOX_HEADS = FOX_WIDTH // HEAD_DIM
RET_HEADS = RET_WIDTH // HEAD_DIM
CONV_K = 3
ROPE_BASE = 10000.0
LN_EPS = 1e-5
GN_EPS = 1e-6

FOX_OFF = 3 * CONV_WIDTH
FLOGIT_OFF = FOX_OFF + 3 * FOX_WIDTH
RET_OFF = FLOGIT_OFF + FOX_HEADS
MAIN_WIDTH = 3 * CONV_WIDTH + 3 * FOX_WIDTH + 4 * RET_WIDTH
RET_MAIN_OFF = FOX_OFF + 3 * FOX_WIDTH

LANE = 128
LOG2_E = 1.4426950408889634

FOX_TQ = 1024
FOX_TK = 512
FOX_HP = 2
VMEM_LIMIT = 56 * 1024 * 1024

BF16 = jnp.bfloat16
F32 = jnp.float32


def _params(*sem):
    return pltpu.CompilerParams(dimension_semantics=sem, vmem_limit_bytes=VMEM_LIMIT)


def _ada_kernel(c_ref, w_ref, b_ref, o_ref):
    c = c_ref[...]
    cond = (c / (1.0 + jnp.exp(-c))).astype(BF16)
    acc = jnp.dot(cond, w_ref[...].astype(BF16), preferred_element_type=F32)
    o_ref[...] = acc + b_ref[...]


def _ada(c_pad, w_ada, b_ada, tn=512):
    depth, d, n = w_ada.shape
    rows = c_pad.shape[0]
    return pl.pallas_call(
        _ada_kernel,
        grid=(depth, n // tn),
        in_specs=[
            pl.BlockSpec((rows, d), lambda l, j: (0, 0)),
            pl.BlockSpec((None, d, tn), lambda l, j: (l, 0, j)),
            pl.BlockSpec((None, 1, tn), lambda l, j: (l, 0, j)),
        ],
        out_specs=pl.BlockSpec((None, rows, tn), lambda l, j: (l, 0, j)),
        out_shape=jax.ShapeDtypeStruct((depth, rows, n), F32),
        compiler_params=_params("parallel", "parallel"),
        name="ada_modulation",
    )(c_pad, w_ada, b_ada.reshape(depth, 1, n))


def _modulate_kernel(x_ref, sc_ref, sh_ref, h_ref):
    h_ref[...] = (x_ref[...] * (1.0 + sc_ref[...]) + sh_ref[...]).astype(h_ref.dtype)


def _modulate(x2d, sc, sh, seq, tm=512):
    m, d = x2d.shape
    per_b = seq // tm
    vec = pl.BlockSpec((None, 1, d), lambda i: (i // per_b, 0, 0))
    return pl.pallas_call(
        _modulate_kernel,
        grid=(m // tm,),
        in_specs=[pl.BlockSpec((tm, d), lambda i: (i, 0)), vec, vec],
        out_specs=pl.BlockSpec((tm, d), lambda i: (i, 0)),
        out_shape=jax.ShapeDtypeStruct((m, d), BF16),
        compiler_params=_params("parallel"),
        name="modulate",
    )(x2d, sc, sh)


def _mm_nt_kernel(a_ref, wt_ref, o_ref):
    acc = lax.dot_general(a_ref[...], wt_ref[...], (((1,), (1,)), ((), ())),
                          preferred_element_type=F32)
    o_ref[...] = acc.astype(o_ref.dtype)


def _matmul_nt(a, wt, layer, *, tm, tn, out_dtype, name):
    m, kdim = a.shape
    n = wt.shape[1]
    return pl.pallas_call(
        _mm_nt_kernel,
        grid=(m // tm, n // tn),
        in_specs=[
            pl.BlockSpec((tm, kdim), lambda i, j: (i, 0)),
            pl.BlockSpec((None, tn, kdim), lambda i, j: (layer, j, 0)),
        ],
        out_specs=pl.BlockSpec((tm, tn), lambda i, j: (i, j)),
        out_shape=jax.ShapeDtypeStruct((m, n), out_dtype),
        compiler_params=_params("parallel", "parallel"),
        name=name,
    )(a, wt)


def _mm_kernel(a_ref, w_ref, o_ref, *scratch, nk, relu2):
    part = jnp.dot(a_ref[...], w_ref[...], preferred_element_type=F32)

    def finish(acc):
        if relu2:
            acc = jnp.square(jnp.maximum(acc, 0.0))
        o_ref[...] = acc.astype(o_ref.dtype)

    if nk == 1:
        finish(part)
        return
    acc_ref, = scratch
    k = pl.program_id(2)

    @pl.when(k == 0)
    def _():
        acc_ref[...] = part

    @pl.when(jnp.logical_and(k > 0, k < nk - 1))
    def _():
        acc_ref[...] += part

    @pl.when(k == nk - 1)
    def _():
        finish(acc_ref[...] + part)


def _matmul(a, w, *, tm, tn, tk, out_dtype, layer=None, relu2=False, name="matmul"):
    m, kdim = a.shape
    n = w.shape[-1]
    nk = kdim // tk
    scratch = [] if nk == 1 else [pltpu.VMEM((tm, tn), F32)]
    if layer is None:
        w_spec = pl.BlockSpec((tk, tn), lambda i, j, k: (k, j))
    else:
        w_spec = pl.BlockSpec((None, tk, tn), lambda i, j, k: (layer, k, j))
    return pl.pallas_call(
        functools.partial(_mm_kernel, nk=nk, relu2=relu2),
        grid=(m // tm, n // tn, nk),
        in_specs=[
            pl.BlockSpec((tm, tk), lambda i, j, k: (i, k)),
            w_spec,
        ],
        out_specs=pl.BlockSpec((tm, tn), lambda i, j, k: (i, j)),
        out_shape=jax.ShapeDtypeStruct((m, n), out_dtype),
        scratch_shapes=scratch,
        compiler_params=_params("parallel", "parallel", "arbitrary"),
        name=name,
    )(a, w)


def _mlp_up_kernel(h_ref, w_ref, wd_ref, a_ref, wd_bf16_ref):
    acc = jnp.dot(h_ref[...], w_ref[...].astype(BF16), preferred_element_type=F32)
    a_ref[...] = jnp.square(jnp.maximum(acc, 0.0)).astype(a_ref.dtype)
    wd_bf16_ref[...] = wd_ref[...].astype(wd_bf16_ref.dtype)


def _mlp_up(h, w_up, w_down, layer, tm=1024, tn=512):
    m, d = h.shape
    d_ff = w_up.shape[-1]
    n_i, n_j = m // tm, d_ff // tn
    slab = d_ff // (n_i * n_j)
    return pl.pallas_call(
        _mlp_up_kernel,
        grid=(n_i, n_j),
        in_specs=[
            pl.BlockSpec((tm, d), lambda i, j: (i, 0)),
            pl.BlockSpec((None, d, tn), lambda i, j: (layer, 0, j)),
            pl.BlockSpec((None, slab, d), lambda i, j: (layer, i * n_j + j, 0)),
        ],
        out_specs=[
            pl.BlockSpec((tm, tn), lambda i, j: (i, j)),
            pl.BlockSpec((slab, d), lambda i, j: (i * n_j + j, 0)),
        ],
        out_shape=[jax.ShapeDtypeStruct((m, d_ff), BF16), jax.ShapeDtypeStruct((d_ff, d), BF16)],
        compiler_params=_params("parallel", "parallel"),
        name="mlp_up",
    )(h, w_up, w_down)


def _out_proj_kernel(yc_ref, yf_ref, yr_ref, w_ref, o_ref):
    f0 = yc_ref.shape[1]
    r0 = f0 + yf_ref.shape[1]
    acc = jnp.dot(yc_ref[...], w_ref[0:f0, :].astype(BF16), preferred_element_type=F32)
    acc += jnp.dot(yf_ref[...], w_ref[f0:r0, :].astype(BF16), preferred_element_type=F32)
    acc += jnp.dot(yr_ref[...], w_ref[r0:, :].astype(BF16), preferred_element_type=F32)
    o_ref[...] = acc


def _out_proj(y_conv, y_fox, y_ret, w, layer, tm=1024, tn=512):
    m = y_conv.shape[0]
    _, kdim, n = w.shape

    def lhs(width):
        return pl.BlockSpec((tm, width), lambda i, j: (i, 0))

    return pl.pallas_call(
        _out_proj_kernel,
        grid=(m // tm, n // tn),
        in_specs=[lhs(y_conv.shape[1]), lhs(y_fox.shape[1]), lhs(y_ret.shape[1]),
                  pl.BlockSpec((None, kdim, tn), lambda i, j: (layer, 0, j))],
        out_specs=pl.BlockSpec((tm, tn), lambda i, j: (i, j)),
        out_shape=jax.ShapeDtypeStruct((m, n), F32),
        compiler_params=_params("parallel", "parallel"),
        name="out_proj",
    )(y_conv, y_fox, y_ret, w)


def _conv_kernel(u_ref, b_ref, c_ref, w_ref, o_ref, *, rows):
    seq, tc = u_ref.shape
    w0 = w_ref[0:1, :]
    w1 = w_ref[1:2, :]
    w2 = w_ref[2:3, :]
    rid = lax.broadcasted_iota(jnp.int32, (rows, 1), 0)

    def chunk(ci, carry):
        prev2, prev1 = carry
        r0 = pl.multiple_of(ci * rows, rows)
        z = c_ref[pl.ds(r0, rows), :].astype(F32) * u_ref[pl.ds(r0, rows), :].astype(F32)
        z1 = jnp.where(rid == 0, prev1, pltpu.roll(z, 1, 0))
        z2 = jnp.where(rid == 0, prev2, jnp.where(rid == 1, prev1, pltpu.roll(z, 2, 0)))
        y = w0 * z2 + w1 * z1 + w2 * z
        o_ref[pl.ds(r0, rows), :] = (b_ref[pl.ds(r0, rows), :].astype(F32) * y).astype(o_ref.dtype)
        return z[rows - 2:rows - 1, :], z[rows - 1:rows, :]

    zero = jnp.zeros((1, tc), F32)
    lax.fori_loop(0, seq // rows, chunk, (zero, zero))


def _short_conv(proj, conv_w, batch, seq, tc=256, rows=512):
    nblk = CONV_WIDTH // tc
    return pl.pallas_call(
        functools.partial(_conv_kernel, rows=rows),
        grid=(batch, nblk),
        in_specs=[
            pl.BlockSpec((seq, tc), lambda b, j: (b, j)),
            pl.BlockSpec((seq, tc), lambda b, j: (b, nblk + j)),
            pl.BlockSpec((seq, tc), lambda b, j: (b, 2 * nblk + j)),
            pl.BlockSpec((CONV_K, tc), lambda b, j: (0, j)),
        ],
        out_specs=pl.BlockSpec((seq, tc), lambda b, j: (b, j)),
        out_shape=jax.ShapeDtypeStruct((batch * seq, CONV_WIDTH), BF16),
        compiler_params=_params("parallel", "parallel"),
        name="short_conv",
    )(proj, proj, proj, conv_w)


def _cum_kernel(f_ref, bf_ref, o_ref, carry_ref):
    rows = f_ref.shape[0]

    @pl.when(pl.program_id(1) == 0)
    def _():
        carry_ref[...] = jnp.zeros_like(carry_ref)

    z = f_ref[...] + bf_ref[...]
    logf = jnp.minimum(z, 0.0) - jnp.log1p(jnp.exp(-jnp.abs(z)))
    p1 = logf.astype(BF16)
    r1 = logf - p1.astype(F32)
    p2 = r1.astype(BF16)
    p3 = (r1 - p2.astype(F32)).astype(BF16)
    ri = lax.broadcasted_iota(jnp.int32, (rows, rows), 0)
    ci = lax.broadcasted_iota(jnp.int32, (rows, rows), 1)
    tri = (ri >= ci).astype(BF16)
    cum = (jnp.dot(tri, p1, preferred_element_type=F32)
           + jnp.dot(tri, p2, preferred_element_type=F32)
           + jnp.dot(tri, p3, preferred_element_type=F32)) + carry_ref[...]
    o_ref[...] = cum.T
    carry_ref[...] = cum[rows - 1:rows, :]


def _forget_cumsum(f_logit, b_f_pad, batch, seq, rows):
    per_b = seq // rows
    return pl.pallas_call(
        _cum_kernel,
        grid=(batch, per_b),
        in_specs=[
            pl.BlockSpec((rows, LANE), lambda b, j: (b * per_b + j, 0)),
            pl.BlockSpec((1, LANE), lambda b, j: (0, 0)),
        ],
        out_specs=pl.BlockSpec((None, None, LANE, rows), lambda b, j: (b, j, 0, 0)),
        out_shape=jax.ShapeDtypeStruct((batch, per_b, LANE, rows), F32),
        scratch_shapes=[pltpu.VMEM((1, LANE), F32)],
        compiler_params=_params("parallel", "arbitrary"),
        name="forget_cumsum",
    )(f_logit, b_f_pad)


def _fox_kernel(q_ref, k_ref, v_ref, f_ref, o_ref, vt_ref, frep_ref, m_ref, l_ref, acc_ref, *, tq, tk, hp):
    seq = q_ref.shape[0]
    inv_scale = HEAD_DIM ** 0.5
    exp2_scale = HEAD_DIM ** -0.5 * LOG2_E
    head0 = pl.program_id(1) * hp
    per_q = tq // tk
    rel_pos = (lax.broadcasted_iota(jnp.int32, (tk, tq), 1)
               - lax.broadcasted_iota(jnp.int32, (tk, tq), 0))

    def head_cols(hh):
        return slice(hh * HEAD_DIM, (hh + 1) * HEAD_DIM)

    for hh in range(hp):
        def prepare(j, carry, hh=hh):
            r0 = pl.multiple_of(j * tk, tk)
            vt_ref[hh, j] = v_ref[pl.ds(r0, tk), head_cols(hh)].astype(F32).T.astype(BF16)
            f_row = f_ref[j, pl.ds(head0 + hh, 1), :] * inv_scale
            frep_ref[hh, j] = jnp.broadcast_to(f_row, (LANE, tk)).T
            return carry

        lax.fori_loop(0, seq // tk, prepare, 0)

    def q_block(qi, _):
        q0 = pl.multiple_of(qi * tq, tq)
        m_ref[...] = jnp.full_like(m_ref, -jnp.inf)
        l_ref[...] = jnp.zeros_like(l_ref)
        acc_ref[...] = jnp.zeros_like(acc_ref)

        def kv_block(ki, diag):
            k0 = pl.multiple_of(ki * tk, tk)
            for hh in range(hp):
                q = q_ref[pl.ds(q0, tq), head_cols(hh)]
                k = k_ref[pl.ds(k0, tk), head_cols(hh)]
                st = lax.dot_general(k, q, (((1,), (1,)), ((), ())), preferred_element_type=F32)
                u = st - jnp.concatenate([frep_ref[hh, ki]] * (tq // LANE), axis=1)
                if diag is not None:
                    u = jnp.where(rel_pos >= diag * tk, u, -jnp.inf)
                m_old = m_ref[hh]
                m_new = jnp.maximum(m_old, jnp.max(u, axis=0, keepdims=True))
                p = jnp.exp2((u - m_new) * exp2_scale)
                alpha = jnp.exp2((m_old - m_new) * exp2_scale)
                l_ref[hh] = alpha * l_ref[hh] + jnp.sum(p, axis=0, keepdims=True)
                acc_ref[hh] = alpha * acc_ref[hh] + jnp.dot(
                    vt_ref[hh, ki], p.astype(BF16), preferred_element_type=F32)
                m_ref[hh] = m_new

        def full_block(ki, carry):
            kv_block(ki, None)
            return carry

        lax.fori_loop(0, qi * per_q, full_block, 0)
        for d in range(per_q):
            kv_block(qi * per_q + d, d)
        for hh in range(hp):
            o_ref[pl.ds(q0, tq), head_cols(hh)] = (acc_ref[hh] / l_ref[hh]).T.astype(o_ref.dtype)
        return 0

    lax.fori_loop(0, seq // tq, q_block, 0)


def _forgetting_attention(proj, cum, batch, seq, tq, tk, hp):
    groups = FOX_HEADS // hp
    width = hp * HEAD_DIM
    qoff = FOX_OFF // width
    fox_blocks = FOX_WIDTH // width
    head_rows = 16

    def head_spec(off):
        return pl.BlockSpec((seq, width), lambda b, g: (b, off + g))

    return pl.pallas_call(
        functools.partial(_fox_kernel, tq=tq, tk=tk, hp=hp),
        grid=(batch, groups),
        in_specs=[
            head_spec(qoff), head_spec(qoff + fox_blocks), head_spec(qoff + 2 * fox_blocks),
            pl.BlockSpec((None, seq // tk, head_rows, tk), lambda b, g: (b, 0, 0, 0)),
        ],
        out_specs=pl.BlockSpec((seq, width), lambda b, g: (b, g)),
        out_shape=jax.ShapeDtypeStruct((batch * seq, FOX_WIDTH), BF16),
        scratch_shapes=[
            pltpu.VMEM((hp, seq // tk, HEAD_DIM, tk), BF16),
            pltpu.VMEM((hp, seq // tk, tk, LANE), F32),
            pltpu.VMEM((hp, 1, tq), F32), pltpu.VMEM((hp, 1, tq), F32),
            pltpu.VMEM((hp, HEAD_DIM, tq), F32),
        ],
        compiler_params=_params("parallel", "parallel"),
        name="forgetting_attention",
    )(proj, proj, proj, cum)


def _ret_kernel(lg_ref, q_ref, k_ref, v_ref, g_ref, cos_ref, sin_ref, o_ref, *, chunk, hp):
    seq = q_ref.shape[0]
    scale = HEAD_DIM ** -0.5
    half = HEAD_DIM // 2
    ri = lax.broadcasted_iota(jnp.int32, (chunk, chunk), 0)
    ci = lax.broadcasted_iota(jnp.int32, (chunk, chunk), 1)
    diff = (ri - ci).astype(F32)
    jv = lax.broadcasted_iota(jnp.int32, (chunk, 1), 0).astype(F32)
    decays = []
    for hh in range(hp):
        lg = lg_ref[pl.program_id(1) * hp + hh]
        decays.append((
            jnp.where(diff >= 0.0, jnp.exp(jnp.maximum(diff, 0.0) * lg), 0.0),
            jnp.exp((jv + 1.0) * lg),
            jnp.exp((chunk - 1.0 - jv) * lg),
            jnp.exp(jnp.full((1, 1), float(chunk), F32) * lg),
        ))

    def rotate(t, cosf, sinf):
        return t * cosf + pltpu.roll(t, half, 1) * sinf

    def step(c, states):
        r0 = pl.multiple_of(c * chunk, chunk)
        rows = pl.ds(r0, chunk)
        cosf = cos_ref[rows, :]
        sinf = sin_ref[rows, :]
        new_states = []
        for hh in range(hp):
            cols = slice(hh * HEAD_DIM, (hh + 1) * HEAD_DIM)
            intra, q_decay, k_decay, chunk_decay = decays[hh]
            state = states[hh]
            q = rotate(q_ref[rows, cols].astype(F32), cosf, sinf)
            k = rotate(k_ref[rows, cols].astype(F32), cosf, sinf) * scale
            v = v_ref[rows, cols]
            scores = lax.dot_general(q.astype(BF16), k.astype(BF16), (((1,), (1,)), ((), ())),
                                     preferred_element_type=F32) * intra
            out = (jnp.dot(scores.astype(BF16), v, preferred_element_type=F32)
                   + jnp.dot((q * q_decay).astype(BF16), state.astype(BF16), preferred_element_type=F32))
            new_states.append(state * chunk_decay + lax.dot_general(
                (k * k_decay).astype(BF16), v, (((0,), (0,)), ((), ())), preferred_element_type=F32))
            mu = jnp.mean(out, axis=1, keepdims=True)
            cen = out - mu
            var = jnp.mean(cen * cen, axis=1, keepdims=True)
            y = cen * lax.rsqrt(var + GN_EPS)
            g = g_ref[rows, cols].astype(F32)
            o_ref[rows, cols] = (y * (g / (1.0 + jnp.exp(-g)))).astype(o_ref.dtype)
        return tuple(new_states)

    zero = jnp.zeros((HEAD_DIM, HEAD_DIM), F32)
    lax.fori_loop(0, seq // chunk, step, (zero,) * hp)


def _retention(proj, log_gamma, cosf, sinf, batch, seq, chunk=256, hp=2):
    groups = RET_HEADS // hp
    width = hp * HEAD_DIM
    qoff = RET_MAIN_OFF // width
    ret_blocks = RET_WIDTH // width

    def head_spec(off):
        return pl.BlockSpec((seq, width), lambda b, g, lg: (b, off + g))

    table = pl.BlockSpec((seq, HEAD_DIM), lambda b, g, lg: (0, 0))
    return pl.pallas_call(
        functools.partial(_ret_kernel, chunk=chunk, hp=hp),
        grid_spec=pltpu.PrefetchScalarGridSpec(
            num_scalar_prefetch=1,
            grid=(batch, groups),
            in_specs=[head_spec(qoff), head_spec(qoff + ret_blocks), head_spec(qoff + 2 * ret_blocks),
                      head_spec(qoff + 3 * ret_blocks), table, table],
            out_specs=pl.BlockSpec((seq, width), lambda b, g, lg: (b, g)),
        ),
        out_shape=jax.ShapeDtypeStruct((batch * seq, RET_WIDTH), BF16),
        compiler_params=_params("parallel", "parallel"),
        name="retention",
    )(log_gamma, proj, proj, proj, proj, cosf, sinf)


def _ln_kernel(x_ref, y_ref, g_ref, gain_ref, bias_ref, *rest, alpha, modulate):
    r = alpha * x_ref[...] + g_ref[...] * y_ref[...]
    mu = jnp.mean(r, axis=1, keepdims=True)
    cen = r - mu
    var = jnp.mean(cen * cen, axis=1, keepdims=True)
    xn = cen * lax.rsqrt(var + LN_EPS) * gain_ref[...] + bias_ref[...]
    if modulate:
        sc_ref, sh_ref, xo_ref, h_ref = rest
        xo_ref[...] = xn
        h_ref[...] = (xn * (1.0 + sc_ref[...]) + sh_ref[...]).astype(h_ref.dtype)
    else:
        xo_ref, = rest
        xo_ref[...] = xn


def _residual_ln(x2d, branch, gate, gain, bias, sc, sh, seq, alpha, tm=256):
    m, d = x2d.shape
    per_b = seq // tm
    modulate = sc is not None
    row = pl.BlockSpec((tm, d), lambda i: (i, 0))
    vec_b = pl.BlockSpec((None, 1, d), lambda i: (i // per_b, 0, 0))
    vec = pl.BlockSpec((1, d), lambda i: (0, 0))
    in_specs = [row, row, vec_b, vec, vec]
    args = [x2d, branch, gate, gain, bias]
    out_specs = [row]
    out_shape = [jax.ShapeDtypeStruct((m, d), F32)]
    if modulate:
        in_specs += [vec_b, vec_b]
        args += [sc, sh]
        out_specs.append(row)
        out_shape.append(jax.ShapeDtypeStruct((m, d), BF16))
    res = pl.pallas_call(
        functools.partial(_ln_kernel, alpha=alpha, modulate=modulate),
        grid=(m // tm,),
        in_specs=in_specs,
        out_specs=out_specs,
        out_shape=out_shape,
        compiler_params=_params("parallel"),
        name="residual_layernorm",
    )(*args)
    return (res[0], res[1]) if modulate else (res[0], None)


def _rotary_tables(seq):
    pos = np.arange(seq, dtype=np.float32)
    inv_freq = (np.float32(ROPE_BASE) ** (-np.arange(0, HEAD_DIM, 2, dtype=np.float32) / np.float32(HEAD_DIM))
                ).astype(np.float32)
    ang = (pos[:, None] * inv_freq[None, :]).astype(np.float32)
    cos = np.cos(ang).astype(np.float32)
    sin = np.sin(ang).astype(np.float32)
    return (jnp.asarray(np.concatenate([cos, cos], axis=1)),
            jnp.asarray(np.concatenate([-sin, sin], axis=1)))


def _log_gamma():
    h = np.arange(RET_HEADS, dtype=np.float32)
    return jnp.asarray(np.log1p(-np.exp2(np.float32(-5.0) - h)).astype(np.float32))


def kernel(x, c, w_ada, b_ada, w_in, b_f, conv_w, w_out, ln1_g, ln1_b, w_up, w_down, ln2_g, ln2_b):
    batch, seq, d = x.shape
    depth = w_ada.shape[0]
    m = batch * seq
    alpha = (2.0 * depth) ** 0.25
    pad_rows = 8

    c_pad = jnp.zeros((pad_rows, d), F32).at[:batch].set(c)
    mod = _ada(c_pad, w_ada, b_ada)
    mod = mod[:, :batch].reshape(depth, batch, 6, 1, d)

    cosf, sinf = _rotary_tables(seq)
    log_gamma = _log_gamma()

    w_in_t = jnp.swapaxes(w_in, 1, 2)
    w_main_t = jnp.concatenate([w_in_t[:, :FLOGIT_OFF], w_in_t[:, RET_OFF:]], axis=1).astype(BF16)
    w_f_t = jnp.zeros((depth, LANE, d), BF16).at[:, :FOX_HEADS].set(
        w_in_t[:, FLOGIT_OFF:RET_OFF].astype(BF16))
    b_f_pad = jnp.zeros((depth, 1, LANE), F32).at[:, 0, :FOX_HEADS].set(b_f)

    x2d = x.reshape(m, d)
    h = _modulate(x2d, mod[0, :, 1], mod[0, :, 0], seq)
    for l in range(depth):
        sh_m, sc_m, g_a, g_m = mod[l, :, 3], mod[l, :, 4], mod[l, :, 2], mod[l, :, 5]

        proj = _matmul_nt(h, w_main_t, l, tm=1024, tn=1536, out_dtype=BF16, name="in_proj")
        f_logit = _matmul_nt(h, w_f_t, l, tm=1024, tn=LANE, out_dtype=F32, name="forget_logits")

        cum = _forget_cumsum(f_logit, b_f_pad[l], batch, seq, rows=FOX_TK)
        y_conv = _short_conv(proj, conv_w[l], batch, seq)
        y_fox = _forgetting_attention(proj, cum, batch, seq, tq=FOX_TQ, tk=FOX_TK, hp=FOX_HP)
        y_ret = _retention(proj, log_gamma, cosf, sinf, batch, seq)

        mix = _out_proj(y_conv, y_fox, y_ret, w_out, l)
        x2d, h2 = _residual_ln(x2d, mix, g_a, ln1_g[l][None], ln1_b[l][None], sc_m, sh_m, seq, alpha)

        a, w_down_l = _mlp_up(h2, w_up, w_down, l)
        mlp = _matmul(a, w_down_l, tm=1024, tn=1024, tk=d, out_dtype=F32, name="mlp_down")
        if l + 1 < depth:
            x2d, h = _residual_ln(x2d, mlp, g_m, ln2_g[l][None], ln2_b[l][None],
                                  mod[l + 1, :, 1], mod[l + 1, :, 0], seq, alpha)
        else:
            x2d, _ = _residual_ln(x2d, mlp, g_m, ln2_g[l][None], ln2_b[l][None], None, None, seq, alpha)
    return x2d.reshape(batch, seq, d)
```

```python
import functools

import numpy as np
import jax
import jax.numpy as jnp
from jax import lax
from jax.experimental import pallas as pl
from jax.experimental.pallas import tpu as pltpu

D_MODEL = 4096
HEAD_DIM = 128
CONV_WIDTH = D_MODEL // 4
FOX_WIDTH = (D_MODEL - CONV_WIDTH) // 2
RET_WIDTH = D_MODEL - CONV_WIDTH - FOX_WIDTH
FOX_HEADS = FOX_WIDTH // HEAD_DIM
RET_HEADS = RET_WIDTH // HEAD_DIM
CONV_K = 3
ROPE_BASE = 10000.0
LN_EPS = 1e-5
GN_EPS = 1e-6

FOX_OFF = 3 * CONV_WIDTH
FLOGIT_OFF = FOX_OFF + 3 * FOX_WIDTH
RET_OFF = FLOGIT_OFF + FOX_HEADS
MAIN_WIDTH = 3 * CONV_WIDTH + 3 * FOX_WIDTH + 4 * RET_WIDTH
RET_MAIN_OFF = FOX_OFF + 3 * FOX_WIDTH

LANE = 128
LOG2_E = 1.4426950408889634

FOX_TQ = 1024
FOX_TK = 512
FOX_HP = 2
VMEM_LIMIT = 56 * 1024 * 1024

BF16 = jnp.bfloat16
F32 = jnp.float32


def _params(*sem):
    return pltpu.CompilerParams(dimension_semantics=sem, vmem_limit_bytes=VMEM_LIMIT)


def _ada_kernel(c_ref, w_ref, b_ref, o_ref):
    c = c_ref[...]
    cond = (c / (1.0 + jnp.exp(-c))).astype(BF16)
    acc = jnp.dot(cond, w_ref[...].astype(BF16), preferred_element_type=F32)
    o_ref[...] = acc + b_ref[...]


def _ada(c_pad, w_ada, b_ada, tn=512):
    depth, d, n = w_ada.shape
    rows = c_pad.shape[0]
    return pl.pallas_call(
        _ada_kernel,
        grid=(depth, n // tn),
        in_specs=[
            pl.BlockSpec((rows, d), lambda l, j: (0, 0)),
            pl.BlockSpec((None, d, tn), lambda l, j: (l, 0, j)),
            pl.BlockSpec((None, 1, tn), lambda l, j: (l, 0, j)),
        ],
        out_specs=pl.BlockSpec((None, rows, tn), lambda l, j: (l, 0, j)),
        out_shape=jax.ShapeDtypeStruct((depth, rows, n), F32),
        compiler_params=_params("parallel", "parallel"),
        name="ada_modulation",
    )(c_pad, w_ada, b_ada.reshape(depth, 1, n))


def _modulate_kernel(x_ref, sc_ref, sh_ref, h_ref):
    h_ref[...] = (x_ref[...] * (1.0 + sc_ref[...]) + sh_ref[...]).astype(h_ref.dtype)


def _modulate(x2d, sc, sh, seq, tm=512):
    m, d = x2d.shape
    per_b = seq // tm
    vec = pl.BlockSpec((None, 1, d), lambda i: (i // per_b, 0, 0))
    return pl.pallas_call(
        _modulate_kernel,
        grid=(m // tm,),
        in_specs=[pl.BlockSpec((tm, d), lambda i: (i, 0)), vec, vec],
        out_specs=pl.BlockSpec((tm, d), lambda i: (i, 0)),
        out_shape=jax.ShapeDtypeStruct((m, d), BF16),
        compiler_params=_params("parallel"),
        name="modulate",
    )(x2d, sc, sh)


def _prep_main_kernel(a_ref, b_ref, o_ref, *, first_shifted, shift):
    @pl.when(pl.program_id(0) < first_shifted)
    def _():
        o_ref[...] = jnp.swapaxes(a_ref[...], 0, 1).astype(o_ref.dtype)

    @pl.when(pl.program_id(0) >= first_shifted)
    def _():
        rows = jnp.concatenate([a_ref[shift:, :, :], b_ref[:shift, :, :]], axis=0)
        o_ref[...] = jnp.swapaxes(rows, 0, 1).astype(o_ref.dtype)


def _prep_flogit_kernel(a_ref, o_ref, *, count):
    tail = a_ref.shape[0]
    v = jnp.swapaxes(a_ref[...], 0, 1)
    keep = lax.broadcasted_iota(jnp.int32, v.shape, 1) < count
    o_ref[...] = jnp.zeros_like(o_ref)
    o_ref[:, :tail, :] = jnp.where(keep, v, 0.0).astype(o_ref.dtype)


def _prep_in_proj_weights(w_in, rows=256, tail=16):
    depth, d, _ = w_in.shape
    w_nlk = jnp.transpose(w_in, (2, 0, 1))
    shift = RET_OFF - FLOGIT_OFF
    w_main_t = pl.pallas_call(
        functools.partial(_prep_main_kernel, first_shifted=FLOGIT_OFF // rows, shift=shift),
        grid=(MAIN_WIDTH // rows,),
        in_specs=[
            pl.BlockSpec((rows, depth, d), lambda i: (i, 0, 0)),
            pl.BlockSpec((tail, depth, d), lambda i: ((i + 1) * (rows // tail), 0, 0)),
        ],
        out_specs=pl.BlockSpec((depth, rows, d), lambda i: (0, i, 0)),
        out_shape=jax.ShapeDtypeStruct((depth, MAIN_WIDTH, d), BF16),
        compiler_params=_params("parallel"),
        name="prep_in_proj_weights",
    )(w_nlk, w_nlk)
    w_f_t = pl.pallas_call(
        functools.partial(_prep_flogit_kernel, count=FOX_HEADS),
        grid=(1,),
        in_specs=[pl.BlockSpec((tail, depth, d), lambda i: (FLOGIT_OFF // tail, 0, 0))],
        out_specs=pl.BlockSpec((depth, LANE, d), lambda i: (0, 0, 0)),
        out_shape=jax.ShapeDtypeStruct((depth, LANE, d), BF16),
        compiler_params=_params("arbitrary"),
        name="prep_forget_logit_weights",
    )(w_nlk)
    return w_main_t, w_f_t


def _mm_nt_kernel(a_ref, wt_ref, o_ref):
    acc = lax.dot_general(a_ref[...], wt_ref[...], (((1,), (1,)), ((), ())),
                          preferred_element_type=F32)
    o_ref[...] = acc.astype(o_ref.dtype)


def _matmul_nt(a, wt, layer, *, tm, tn, out_dtype, name):
    m, kdim = a.shape
    n = wt.shape[1]
    return pl.pallas_call(
        _mm_nt_kernel,
        grid=(m // tm, n // tn),
        in_specs=[
            pl.BlockSpec((tm, kdim), lambda i, j: (i, 0)),
            pl.BlockSpec((None, tn, kdim), lambda i, j: (layer, j, 0)),
        ],
        out_specs=pl.BlockSpec((tm, tn), lambda i, j: (i, j)),
        out_shape=jax.ShapeDtypeStruct((m, n), out_dtype),
        compiler_params=_params("parallel", "parallel"),
        name=name,
    )(a, wt)


def _mm_kernel(a_ref, w_ref, o_ref, *scratch, nk, relu2):
    part = jnp.dot(a_ref[...], w_ref[...], preferred_element_type=F32)

    def finish(acc):
        if relu2:
            acc = jnp.square(jnp.maximum(acc, 0.0))
        o_ref[...] = acc.astype(o_ref.dtype)

    if nk == 1:
        finish(part)
        return
    acc_ref, = scratch
    k = pl.program_id(2)

    @pl.when(k == 0)
    def _():
        acc_ref[...] = part

    @pl.when(jnp.logical_and(k > 0, k < nk - 1))
    def _():
        acc_ref[...] += part

    @pl.when(k == nk - 1)
    def _():
        finish(acc_ref[...] + part)


def _matmul(a, w, *, tm, tn, tk, out_dtype, layer=None, relu2=False, name="matmul"):
    m, kdim = a.shape
    n = w.shape[-1]
    nk = kdim // tk
    scratch = [] if nk == 1 else [pltpu.VMEM((tm, tn), F32)]
    if layer is None:
        w_spec = pl.BlockSpec((tk, tn), lambda i, j, k: (k, j))
    else:
        w_spec = pl.BlockSpec((None, tk, tn), lambda i, j, k: (layer, k, j))
    return pl.pallas_call(
        functools.partial(_mm_kernel, nk=nk, relu2=relu2),
        grid=(m // tm, n // tn, nk),
        in_specs=[
            pl.BlockSpec((tm, tk), lambda i, j, k: (i, k)),
            w_spec,
        ],
        out_specs=pl.BlockSpec((tm, tn), lambda i, j, k: (i, j)),
        out_shape=jax.ShapeDtypeStruct((m, n), out_dtype),
        scratch_shapes=scratch,
        compiler_params=_params("parallel", "parallel", "arbitrary"),
        name=name,
    )(a, w)


def _mlp_up_kernel(h_ref, w_ref, wd_ref, a_ref, wd_bf16_ref):
    acc = jnp.dot(h_ref[...], w_ref[...].astype(BF16), preferred_element_type=F32)
    a_ref[...] = jnp.square(jnp.maximum(acc, 0.0)).astype(a_ref.dtype)
    wd_bf16_ref[...] = wd_ref[...].astype(wd_bf16_ref.dtype)


def _mlp_up(h, w_up, w_down, layer, tm=1024, tn=512):
    m, d = h.shape
    d_ff = w_up.shape[-1]
    n_i, n_j = m // tm, d_ff // tn
    slab = d_ff // (n_i * n_j)
    return pl.pallas_call(
        _mlp_up_kernel,
        grid=(n_i, n_j),
        in_specs=[
            pl.BlockSpec((tm, d), lambda i, j: (i, 0)),
            pl.BlockSpec((None, d, tn), lambda i, j: (layer, 0, j)),
            pl.BlockSpec((None, slab, d), lambda i, j: (layer, i * n_j + j, 0)),
        ],
        out_specs=[
            pl.BlockSpec((tm, tn), lambda i, j: (i, j)),
            pl.BlockSpec((slab, d), lambda i, j: (i * n_j + j, 0)),
        ],
        out_shape=[jax.ShapeDtypeStruct((m, d_ff), BF16), jax.ShapeDtypeStruct((d_ff, d), BF16)],
        compiler_params=_params("parallel", "parallel"),
        name="mlp_up",
    )(h, w_up, w_down)


def _out_proj_kernel(yc_ref, yf_ref, yr_ref, w_ref, o_ref):
    f0 = yc_ref.shape[1]
    r0 = f0 + yf_ref.shape[1]
    acc = jnp.dot(yc_ref[...], w_ref[0:f0, :].astype(BF16), preferred_element_type=F32)
    acc += jnp.dot(yf_ref[...], w_ref[f0:r0, :].astype(BF16), preferred_element_type=F32)
    acc += jnp.dot(yr_ref[...], w_ref[r0:, :].astype(BF16), preferred_element_type=F32)
    o_ref[...] = acc


def _out_proj(y_conv, y_fox, y_ret, w, layer, tm=1024, tn=512):
    m = y_conv.shape[0]
    _, kdim, n = w.shape

    def lhs(width):
        return pl.BlockSpec((tm, width), lambda i, j: (i, 0))

    return pl.pallas_call(
        _out_proj_kernel,
        grid=(m // tm, n // tn),
        in_specs=[lhs(y_conv.shape[1]), lhs(y_fox.shape[1]), lhs(y_ret.shape[1]),
                  pl.BlockSpec((None, kdim, tn), lambda i, j: (layer, 0, j))],
        out_specs=pl.BlockSpec((tm, tn), lambda i, j: (i, j)),
        out_shape=jax.ShapeDtypeStruct((m, n), F32),
        compiler_params=_params("parallel", "parallel"),
        name="out_proj",
    )(y_conv, y_fox, y_ret, w)


def _conv_kernel(u_ref, b_ref, c_ref, w_ref, o_ref, *, rows):
    seq, tc = u_ref.shape
    w0 = w_ref[0:1, :]
    w1 = w_ref[1:2, :]
    w2 = w_ref[2:3, :]
    rid = lax.broadcasted_iota(jnp.int32, (rows, 1), 0)

    def chunk(ci, carry):
        prev2, prev1 = carry
        r0 = pl.multiple_of(ci * rows, rows)
        z = c_ref[pl.ds(r0, rows), :].astype(F32) * u_ref[pl.ds(r0, rows), :].astype(F32)
        z1 = jnp.where(rid == 0, prev1, pltpu.roll(z, 1, 0))
        z2 = jnp.where(rid == 0, prev2, jnp.where(rid == 1, prev1, pltpu.roll(z, 2, 0)))
        y = w0 * z2 + w1 * z1 + w2 * z
        o_ref[pl.ds(r0, rows), :] = (b_ref[pl.ds(r0, rows), :].astype(F32) * y).astype(o_ref.dtype)
        return z[rows - 2:rows - 1, :], z[rows - 1:rows, :]

    zero = jnp.zeros((1, tc), F32)
    lax.fori_loop(0, seq // rows, chunk, (zero, zero))


def _short_conv(proj, conv_w, batch, seq, tc=256, rows=512):
    nblk = CONV_WIDTH // tc
    return pl.pallas_call(
        functools.partial(_conv_kernel, rows=rows),
        grid=(batch, nblk),
        in_specs=[
            pl.BlockSpec((seq, tc), lambda b, j: (b, j)),
            pl.BlockSpec((seq, tc), lambda b, j: (b, nblk + j)),
            pl.BlockSpec((seq, tc), lambda b, j: (b, 2 * nblk + j)),
            pl.BlockSpec((CONV_K, tc), lambda b, j: (0, j)),
        ],
        out_specs=pl.BlockSpec((seq, tc), lambda b, j: (b, j)),
        out_shape=jax.ShapeDtypeStruct((batch * seq, CONV_WIDTH), BF16),
        compiler_params=_params("parallel", "parallel"),
        name="short_conv",
    )(proj, proj, proj, conv_w)


def _cum_kernel(f_ref, bf_ref, o_ref, carry_ref):
    rows = f_ref.shape[0]

    @pl.when(pl.program_id(1) == 0)
    def _():
        carry_ref[...] = jnp.zeros_like(carry_ref)

    z = f_ref[...] + bf_ref[...]
    logf = jnp.minimum(z, 0.0) - jnp.log1p(jnp.exp(-jnp.abs(z)))
    p1 = logf.astype(BF16)
    r1 = logf - p1.astype(F32)
    p2 = r1.astype(BF16)
    p3 = (r1 - p2.astype(F32)).astype(BF16)
    ri = lax.broadcasted_iota(jnp.int32, (rows, rows), 0)
    ci = lax.broadcasted_iota(jnp.int32, (rows, rows), 1)
    tri = (ri >= ci).astype(BF16)
    cum = (jnp.dot(tri, p1, preferred_element_type=F32)
           + jnp.dot(tri, p2, preferred_element_type=F32)
           + jnp.dot(tri, p3, preferred_element_type=F32)) + carry_ref[...]
    o_ref[...] = cum.T
    carry_ref[...] = cum[rows - 1:rows, :]


def _forget_cumsum(f_logit, b_f_pad, batch, seq, rows):
    per_b = seq // rows
    return pl.pallas_call(
        _cum_kernel,
        grid=(batch, per_b),
        in_specs=[
            pl.BlockSpec((rows, LANE), lambda b, j: (b * per_b + j, 0)),
            pl.BlockSpec((1, LANE), lambda b, j: (0, 0)),
        ],
        out_specs=pl.BlockSpec((None, None, LANE, rows), lambda b, j: (b, j, 0, 0)),
        out_shape=jax.ShapeDtypeStruct((batch, per_b, LANE, rows), F32),
        scratch_shapes=[pltpu.VMEM((1, LANE), F32)],
        compiler_params=_params("parallel", "arbitrary"),
        name="forget_cumsum",
    )(f_logit, b_f_pad)


def _fox_kernel(q_ref, k_ref, v_ref, f_ref, o_ref, vt_ref, frep_ref, m_ref, l_ref, acc_ref, *, tq, tk, hp):
    seq = q_ref.shape[0]
    inv_scale = HEAD_DIM ** 0.5
    exp2_scale = HEAD_DIM ** -0.5 * LOG2_E
    head0 = pl.program_id(1) * hp
    per_q = tq // tk
    rel_pos = (lax.broadcasted_iota(jnp.int32, (tk, tq), 1)
               - lax.broadcasted_iota(jnp.int32, (tk, tq), 0))

    def head_cols(hh):
        return slice(hh * HEAD_DIM, (hh + 1) * HEAD_DIM)

    for hh in range(hp):
        def prepare(j, carry, hh=hh):
            r0 = pl.multiple_of(j * tk, tk)
            vt_ref[hh, j] = v_ref[pl.ds(r0, tk), head_cols(hh)].astype(F32).T.astype(BF16)
            f_row = f_ref[j, pl.ds(head0 + hh, 1), :] * inv_scale
            frep_ref[hh, j] = jnp.broadcast_to(f_row, (LANE, tk)).T
            return carry

        lax.fori_loop(0, seq // tk, prepare, 0)

    def q_block(qi, _):
        q0 = pl.multiple_of(qi * tq, tq)
        m_ref[...] = jnp.full_like(m_ref, -jnp.inf)
        l_ref[...] = jnp.zeros_like(l_ref)
        acc_ref[...] = jnp.zeros_like(acc_ref)

        def kv_block(ki, diag):
            k0 = pl.multiple_of(ki * tk, tk)
            for hh in range(hp):
                q = q_ref[pl.ds(q0, tq), head_cols(hh)]
                k = k_ref[pl.ds(k0, tk), head_cols(hh)]
                st = lax.dot_general(k, q, (((1,), (1,)), ((), ())), preferred_element_type=F32)
                u = st - jnp.concatenate([frep_ref[hh, ki]] * (tq // LANE), axis=1)
                if diag is not None:
                    u = jnp.where(rel_pos >= diag * tk, u, -jnp.inf)
                m_old = m_ref[hh]
                m_new = jnp.maximum(m_old, jnp.max(u, axis=0, keepdims=True))
                p = jnp.exp2((u - m_new) * exp2_scale)
                alpha = jnp.exp2((m_old - m_new) * exp2_scale)
                l_ref[hh] = alpha * l_ref[hh] + jnp.sum(p, axis=0, keepdims=True)
                acc_ref[hh] = alpha * acc_ref[hh] + jnp.dot(
                    vt_ref[hh, ki], p.astype(BF16), preferred_element_type=F32)
                m_ref[hh] = m_new

        def full_block(ki, carry):
            kv_block(ki, None)
            return carry

        lax.fori_loop(0, qi * per_q, full_block, 0)
        for d in range(per_q):
            kv_block(qi * per_q + d, d)
        for hh in range(hp):
            o_ref[pl.ds(q0, tq), head_cols(hh)] = (acc_ref[hh] / l_ref[hh]).T.astype(o_ref.dtype)
        return 0

    lax.fori_loop(0, seq // tq, q_block, 0)


def _forgetting_attention(proj, cum, batch, seq, tq, tk, hp):
    groups = FOX_HEADS // hp
    width = hp * HEAD_DIM
    qoff = FOX_OFF // width
    fox_blocks = FOX_WIDTH // width
    head_rows = 16

    def head_spec(off):
        return pl.BlockSpec((seq, width), lambda b, g: (b, off + g))

    return pl.pallas_call(
        functools.partial(_fox_kernel, tq=tq, tk=tk, hp=hp),
        grid=(batch, groups),
        in_specs=[
            head_spec(qoff), head_spec(qoff + fox_blocks), head_spec(qoff + 2 * fox_blocks),
            pl.BlockSpec((None, seq // tk, head_rows, tk), lambda b, g: (b, 0, 0, 0)),
        ],
        out_specs=pl.BlockSpec((seq, width), lambda b, g: (b, g)),
        out_shape=jax.ShapeDtypeStruct((batch * seq, FOX_WIDTH), BF16),
        scratch_shapes=[
            pltpu.VMEM((hp, seq // tk, HEAD_DIM, tk), BF16),
            pltpu.VMEM((hp, seq // tk, tk, LANE), F32),
            pltpu.VMEM((hp, 1, tq), F32), pltpu.VMEM((hp, 1, tq), F32),
            pltpu.VMEM((hp, HEAD_DIM, tq), F32),
        ],
        compiler_params=_params("parallel", "parallel"),
        name="forgetting_attention",
    )(proj, proj, proj, cum)


def _ret_kernel(lg_ref, q_ref, k_ref, v_ref, g_ref, cos_ref, sin_ref, o_ref, *, chunk, hp):
    seq = q_ref.shape[0]
    scale = HEAD_DIM ** -0.5
    half = HEAD_DIM // 2
    ri = lax.broadcasted_iota(jnp.int32, (chunk, chunk), 0)
    ci = lax.broadcasted_iota(jnp.int32, (chunk, chunk), 1)
    diff = (ri - ci).astype(F32)
    jv = lax.broadcasted_iota(jnp.int32, (chunk, 1), 0).astype(F32)
    decays = []
    for hh in range(hp):
        lg = lg_ref[pl.program_id(1) * hp + hh]
        decays.append((
            jnp.where(diff >= 0.0, jnp.exp(jnp.maximum(diff, 0.0) * lg), 0.0),
            jnp.exp((jv + 1.0) * lg),
            jnp.exp((chunk - 1.0 - jv) * lg),
            jnp.exp(jnp.full((1, 1), float(chunk), F32) * lg),
        ))

    def rotate(t, cosf, sinf):
        return t * cosf + pltpu.roll(t, half, 1) * sinf

    def step(c, states):
        r0 = pl.multiple_of(c * chunk, chunk)
        rows = pl.ds(r0, chunk)
        cosf = cos_ref[rows, :]
        sinf = sin_ref[rows, :]
        new_states = []
        for hh in range(hp):
            cols = slice(hh * HEAD_DIM, (hh + 1) * HEAD_DIM)
            intra, q_decay, k_decay, chunk_decay = decays[hh]
            state = states[hh]
            q = rotate(q_ref[rows, cols].astype(F32), cosf, sinf)
            k = rotate(k_ref[rows, cols].astype(F32), cosf, sinf) * scale
            v = v_ref[rows, cols]
            scores = lax.dot_general(q.astype(BF16), k.astype(BF16), (((1,), (1,)), ((), ())),
                                     preferred_element_type=F32) * intra
            out = (jnp.dot(scores.astype(BF16), v, preferred_element_type=F32)
                   + jnp.dot((q * q_decay).astype(BF16), state.astype(BF16), preferred_element_type=F32))
            new_states.append(state * chunk_decay + lax.dot_general(
                (k * k_decay).astype(BF16), v, (((0,), (0,)), ((), ())), preferred_element_type=F32))
            mu = jnp.mean(out, axis=1, keepdims=True)
            cen = out - mu
            var = jnp.mean(cen * cen, axis=1, keepdims=True)
            y = cen * lax.rsqrt(var + GN_EPS)
            g = g_ref[rows, cols].astype(F32)
            o_ref[rows, cols] = (y * (g / (1.0 + jnp.exp(-g)))).astype(o_ref.dtype)
        return tuple(new_states)

    zero = jnp.zeros((HEAD_DIM, HEAD_DIM), F32)
    lax.fori_loop(0, seq // chunk, step, (zero,) * hp)


def _retention(proj, log_gamma, cosf, sinf, batch, seq, chunk=256, hp=4):
    groups = RET_HEADS // hp
    width = hp * HEAD_DIM
    qoff = RET_MAIN_OFF // width
    ret_blocks = RET_WIDTH // width

    def head_spec(off):
        return pl.BlockSpec((seq, width), lambda b, g, lg: (b, off + g))

    table = pl.BlockSpec((seq, HEAD_DIM), lambda b, g, lg: (0, 0))
    return pl.pallas_call(
        functools.partial(_ret_kernel, chunk=chunk, hp=hp),
        grid_spec=pltpu.PrefetchScalarGridSpec(
            num_scalar_prefetch=1,
            grid=(batch, groups),
            in_specs=[head_spec(qoff), head_spec(qoff + ret_blocks), head_spec(qoff + 2 * ret_blocks),
                      head_spec(qoff + 3 * ret_blocks), table, table],
            out_specs=pl.BlockSpec((seq, width), lambda b, g, lg: (b, g)),
        ),
        out_shape=jax.ShapeDtypeStruct((batch * seq, RET_WIDTH), BF16),
        compiler_params=_params("parallel", "parallel"),
        name="retention",
    )(log_gamma, proj, proj, proj, proj, cosf, sinf)


def _ln_kernel(x_ref, y_ref, g_ref, gain_ref, bias_ref, *rest, alpha, modulate):
    r = alpha * x_ref[...] + g_ref[...] * y_ref[...]
    mu = jnp.mean(r, axis=1, keepdims=True)
    cen = r - mu
    var = jnp.mean(cen * cen, axis=1, keepdims=True)
    xn = cen * lax.rsqrt(var + LN_EPS) * gain_ref[...] + bias_ref[...]
    if modulate:
        sc_ref, sh_ref, xo_ref, h_ref = rest
        xo_ref[...] = xn
        h_ref[...] = (xn * (1.0 + sc_ref[...]) + sh_ref[...]).astype(h_ref.dtype)
    else:
        xo_ref, = rest
        xo_ref[...] = xn


def _residual_ln(x2d, branch, gate, gain, bias, sc, sh, seq, alpha, tm=256):
    m, d = x2d.shape
    per_b = seq // tm
    modulate = sc is not None
    row = pl.BlockSpec((tm, d), lambda i: (i, 0))
    vec_b = pl.BlockSpec((None, 1, d), lambda i: (i // per_b, 0, 0))
    vec = pl.BlockSpec((1, d), lambda i: (0, 0))
    in_specs = [row, row, vec_b, vec, vec]
    args = [x2d, branch, gate, gain, bias]
    out_specs = [row]
    out_shape = [jax.ShapeDtypeStruct((m, d), F32)]
    if modulate:
        in_specs += [vec_b, vec_b]
        args += [sc, sh]
        out_specs.append(row)
        out_shape.append(jax.ShapeDtypeStruct((m, d), BF16))
    res = pl.pallas_call(
        functools.partial(_ln_kernel, alpha=alpha, modulate=modulate),
        grid=(m // tm,),
        in_specs=in_specs,
        out_specs=out_specs,
        out_shape=out_shape,
        compiler_params=_params("parallel"),
        name="residual_layernorm",
    )(*args)
    return (res[0], res[1]) if modulate else (res[0], None)


def _rotary_tables(seq):
    pos = np.arange(seq, dtype=np.float32)
    inv_freq = (np.float32(ROPE_BASE) ** (-np.arange(0, HEAD_DIM, 2, dtype=np.float32) / np.float32(HEAD_DIM))
                ).astype(np.float32)
    ang = (pos[:, None] * inv_freq[None, :]).astype(np.float32)
    cos = np.cos(ang).astype(np.float32)
    sin = np.sin(ang).astype(np.float32)
    return (jnp.asarray(np.concatenate([cos, cos], axis=1)),
            jnp.asarray(np.concatenate([-sin, sin], axis=1)))


def _log_gamma():
    h = np.arange(RET_HEADS, dtype=np.float32)
    return jnp.asarray(np.log1p(-np.exp2(np.float32(-5.0) - h)).astype(np.float32))


def kernel(x, c, w_ada, b_ada, w_in, b_f, conv_w, w_out, ln1_g, ln1_b, w_up, w_down, ln2_g, ln2_b):
    batch, seq, d = x.shape
    depth = w_ada.shape[0]
    m = batch * seq
    alpha = (2.0 * depth) ** 0.25
    pad_rows = 8

    c_pad = jnp.zeros((pad_rows, d), F32).at[:batch].set(c)
    mod = _ada(c_pad, w_ada, b_ada)
    mod = mod[:, :batch].reshape(depth, batch, 6, 1, d)

    cosf, sinf = _rotary_tables(seq)
    log_gamma = _log_gamma()

    w_main_t, w_f_t = _prep_in_proj_weights(w_in)
    b_f_pad = jnp.zeros((depth, 1, LANE), F32).at[:, 0, :FOX_HEADS].set(b_f)

    x2d = x.reshape(m, d)
    h = _modulate(x2d, mod[0, :, 1], mod[0, :, 0], seq)
    for l in range(depth):
        sh_m, sc_m, g_a, g_m = mod[l, :, 3], mod[l, :, 4], mod[l, :, 2], mod[l, :, 5]

        proj = _matmul_nt(h, w_main_t, l, tm=1024, tn=1536, out_dtype=BF16, name="in_proj")
        f_logit = _matmul_nt(h, w_f_t, l, tm=1024, tn=LANE, out_dtype=F32, name="forget_logits")

        cum = _forget_cumsum(f_logit, b_f_pad[l], batch, seq, rows=FOX_TK)
        y_conv = _short_conv(proj, conv_w[l], batch, seq)
        y_fox = _forgetting_attention(proj, cum, batch, seq, tq=FOX_TQ, tk=FOX_TK, hp=FOX_HP)
        y_ret = _retention(proj, log_gamma, cosf, sinf, batch, seq)

        mix = _out_proj(y_conv, y_fox, y_ret, w_out, l)
        x2d, h2 = _residual_ln(x2d, mix, g_a, ln1_g[l][None], ln1_b[l][None], sc_m, sh_m, seq, alpha)

        a, w_down_l = _mlp_up(h2, w_up, w_down, l)
        mlp = _matmul(a, w_down_l, tm=1024, tn=1024, tk=d, out_dtype=F32, name="mlp_down")
        if l + 1 < depth:
            x2d, h = _residual_ln(x2d, mlp, g_m, ln2_g[l][None], ln2_b[l][None],
                                  mod[l + 1, :, 1], mod[l + 1, :, 0], seq, alpha)
        else:
            x2d, _ = _residual_ln(x2d, mlp, g_m, ln2_g[l][None], ln2_b[l][None], None, None, seq, alpha)
    return x2d.reshape(batch, seq, d)
```

```python
import functools

import numpy as np
import jax
import jax.numpy as jnp
from jax import lax
from jax.experimental import pallas as pl
from jax.experimental.pallas import tpu as pltpu

D_MODEL = 4096
HEAD_DIM = 128
CONV_WIDTH = D_MODEL // 4
FOX_WIDTH = (D_MODEL - CONV_WIDTH) // 2
RET_WIDTH = D_MODEL - CONV_WIDTH - FOX_WIDTH
FOX_HEADS = FOX_WIDTH // HEAD_DIM
RET_HEADS = RET_WIDTH // HEAD_DIM
CONV_K = 3
ROPE_BASE = 10000.0
LN_EPS = 1e-5
GN_EPS = 1e-6

FOX_OFF = 3 * CONV_WIDTH
FLOGIT_OFF = FOX_OFF + 3 * FOX_WIDTH
RET_OFF = FLOGIT_OFF + FOX_HEADS
MAIN_WIDTH = 3 * CONV_WIDTH + 3 * FOX_WIDTH + 4 * RET_WIDTH
RET_MAIN_OFF = FOX_OFF + 3 * FOX_WIDTH

LANE = 128
LOG2_E = 1.4426950408889634

FOX_TQ = 1024
FOX_TK = 512
FOX_HP = 2
FOX_QSPLIT = 2
ONES_ROWS = 16
VMEM_LIMIT = 56 * 1024 * 1024

BF16 = jnp.bfloat16
F32 = jnp.float32


def _params(*sem):
    return pltpu.CompilerParams(dimension_semantics=sem, vmem_limit_bytes=VMEM_LIMIT)


def _ada_block(c_ref, w_ref, b_ref):
    c = c_ref[...]
    cond = (c / (1.0 + jnp.exp(-c))).astype(BF16)
    return jnp.dot(cond, w_ref[...].astype(BF16), preferred_element_type=F32) + b_ref[...]


def _ada_kernel(c_ref, w_ref, b_ref, o_ref):
    o_ref[...] = _ada_block(c_ref, w_ref, b_ref)


def _ada_first_layer(c_pad, w_ada, b_ada3, tn=512):
    _, d, n = w_ada.shape
    rows = c_pad.shape[0]
    return pl.pallas_call(
        _ada_kernel,
        grid=(n // tn,),
        in_specs=[
            pl.BlockSpec((rows, d), lambda j: (0, 0)),
            pl.BlockSpec((None, d, tn), lambda j: (0, 0, j)),
            pl.BlockSpec((None, 1, tn), lambda j: (0, 0, j)),
        ],
        out_specs=pl.BlockSpec((rows, tn), lambda j: (0, j)),
        out_shape=jax.ShapeDtypeStruct((rows, n), F32),
        compiler_params=_params("parallel"),
        name="ada_modulation",
    )(c_pad, w_ada, b_ada3)


def _modulate_kernel(x_ref, sc_ref, sh_ref, h_ref):
    h_ref[...] = (x_ref[...] * (1.0 + sc_ref[...]) + sh_ref[...]).astype(h_ref.dtype)


def _modulate(x2d, sc, sh, seq, tm=512):
    m, d = x2d.shape
    per_b = seq // tm
    vec = pl.BlockSpec((None, 1, d), lambda i: (i // per_b, 0, 0))
    return pl.pallas_call(
        _modulate_kernel,
        grid=(m // tm,),
        in_specs=[pl.BlockSpec((tm, d), lambda i: (i, 0)), vec, vec],
        out_specs=pl.BlockSpec((tm, d), lambda i: (i, 0)),
        out_shape=jax.ShapeDtypeStruct((m, d), BF16),
        compiler_params=_params("parallel"),
        name="modulate",
    )(x2d, sc, sh)


def _prep_main_kernel(a_ref, b_ref, o_ref, *, first_shifted, shift):
    @pl.when(pl.program_id(0) < first_shifted)
    def _():
        o_ref[...] = jnp.swapaxes(a_ref[...], 0, 1).astype(o_ref.dtype)

    @pl.when(pl.program_id(0) >= first_shifted)
    def _():
        rows = jnp.concatenate([a_ref[shift:, :, :], b_ref[:shift, :, :]], axis=0)
        o_ref[...] = jnp.swapaxes(rows, 0, 1).astype(o_ref.dtype)


def _prep_flogit_kernel(a_ref, o_ref, *, count):
    tail = a_ref.shape[0]
    v = jnp.swapaxes(a_ref[...], 0, 1)
    keep = lax.broadcasted_iota(jnp.int32, v.shape, 1) < count
    o_ref[...] = jnp.zeros_like(o_ref)
    o_ref[:, :tail, :] = jnp.where(keep, v, 0.0).astype(o_ref.dtype)


def _prep_in_proj_weights(w_in, rows=256, tail=16):
    depth, d, _ = w_in.shape
    w_nlk = jnp.transpose(w_in, (2, 0, 1))
    shift = RET_OFF - FLOGIT_OFF
    w_main_t = pl.pallas_call(
        functools.partial(_prep_main_kernel, first_shifted=FLOGIT_OFF // rows, shift=shift),
        grid=(MAIN_WIDTH // rows,),
        in_specs=[
            pl.BlockSpec((rows, depth, d), lambda i: (i, 0, 0)),
            pl.BlockSpec((tail, depth, d), lambda i: ((i + 1) * (rows // tail), 0, 0)),
        ],
        out_specs=pl.BlockSpec((depth, rows, d), lambda i: (0, i, 0)),
        out_shape=jax.ShapeDtypeStruct((depth, MAIN_WIDTH, d), BF16),
        compiler_params=_params("parallel"),
        name="prep_in_proj_weights",
    )(w_nlk, w_nlk)
    w_f_t = pl.pallas_call(
        functools.partial(_prep_flogit_kernel, count=FOX_HEADS),
        grid=(1,),
        in_specs=[pl.BlockSpec((tail, depth, d), lambda i: (FLOGIT_OFF // tail, 0, 0))],
        out_specs=pl.BlockSpec((depth, LANE, d), lambda i: (0, 0, 0)),
        out_shape=jax.ShapeDtypeStruct((depth, LANE, d), BF16),
        compiler_params=_params("arbitrary"),
        name="prep_forget_logit_weights",
    )(w_nlk)
    return w_main_t, w_f_t


def _mm_nt_kernel(a_ref, wt_ref, o_ref):
    acc = lax.dot_general(a_ref[...], wt_ref[...], (((1,), (1,)), ((), ())),
                          preferred_element_type=F32)
    o_ref[...] = acc.astype(o_ref.dtype)


def _matmul_nt(a, wt, layer, *, tm, tn, out_dtype, name):
    m, kdim = a.shape
    n = wt.shape[1]
    return pl.pallas_call(
        _mm_nt_kernel,
        grid=(m // tm, n // tn),
        in_specs=[
            pl.BlockSpec((tm, kdim), lambda i, j: (i, 0)),
            pl.BlockSpec((None, tn, kdim), lambda i, j: (layer, j, 0)),
        ],
        out_specs=pl.BlockSpec((tm, tn), lambda i, j: (i, j)),
        out_shape=jax.ShapeDtypeStruct((m, n), out_dtype),
        compiler_params=_params("parallel", "parallel"),
        name=name,
    )(a, wt)


def _mm_ksplit_kernel(a_ref, w_ref, o_ref, acc_ref, *, nk):
    part = jnp.dot(a_ref[...], w_ref[...], preferred_element_type=F32)
    k = pl.program_id(2)

    @pl.when(k == 0)
    def _():
        acc_ref[...] = part

    @pl.when(jnp.logical_and(k > 0, k < nk - 1))
    def _():
        acc_ref[...] += part

    @pl.when(k == nk - 1)
    def _():
        o_ref[...] = (acc_ref[...] + part).astype(o_ref.dtype)


def _matmul_ksplit(a, w, *, tm, tn, tk, out_dtype, name):
    m, kdim = a.shape
    n = w.shape[1]
    nk = kdim // tk
    assert nk >= 2
    return pl.pallas_call(
        functools.partial(_mm_ksplit_kernel, nk=nk),
        grid=(m // tm, n // tn, nk),
        in_specs=[
            pl.BlockSpec((tm, tk), lambda i, j, k: (i, k)),
            pl.BlockSpec((tk, tn), lambda i, j, k: (k, j)),
        ],
        out_specs=pl.BlockSpec((tm, tn), lambda i, j, k: (i, j)),
        out_shape=jax.ShapeDtypeStruct((m, n), out_dtype),
        scratch_shapes=[pltpu.VMEM((tm, tn), F32)],
        compiler_params=_params("parallel", "parallel", "arbitrary"),
        name=name,
    )(a, w)


def _mlp_up_kernel(h_ref, w_ref, wd_ref, *rest, ada_steps):
    if ada_steps:
        c_ref, wa_ref, ba_ref, a_ref, wd_bf16_ref, mod_ref = rest
    else:
        a_ref, wd_bf16_ref = rest
    acc = jnp.dot(h_ref[...], w_ref[...].astype(BF16), preferred_element_type=F32)
    a_ref[...] = jnp.square(jnp.maximum(acc, 0.0)).astype(a_ref.dtype)
    wd_bf16_ref[...] = wd_ref[...].astype(wd_bf16_ref.dtype)
    if ada_steps:
        step = pl.program_id(0) * pl.num_programs(1) + pl.program_id(1)

        @pl.when(step < ada_steps)
        def _():
            mod_ref[...] = _ada_block(c_ref, wa_ref, ba_ref)


def _mlp_up(h, w_up, w_down, layer, ada_next=None, tm=2048, tn=512, ada_tn=256):
    m, d = h.shape
    d_ff = w_up.shape[-1]
    n_i, n_j = m // tm, d_ff // tn
    slab = d_ff // (n_i * n_j)
    in_specs = [
        pl.BlockSpec((tm, d), lambda i, j: (i, 0), pipeline_mode=pl.Buffered(1)),
        pl.BlockSpec((None, d, tn), lambda i, j: (layer, 0, j)),
        pl.BlockSpec((None, slab, d), lambda i, j: (layer, i * n_j + j, 0)),
    ]
    out_specs = [
        pl.BlockSpec((tm, tn), lambda i, j: (i, j)),
        pl.BlockSpec((slab, d), lambda i, j: (i * n_j + j, 0)),
    ]
    out_shape = [jax.ShapeDtypeStruct((m, d_ff), BF16), jax.ShapeDtypeStruct((d_ff, d), BF16)]
    args = [h, w_up, w_down]
    ada_steps = 0
    if ada_next is not None:
        c_pad, w_ada, b_ada3 = ada_next
        rows, n_mod = c_pad.shape[0], w_ada.shape[-1]
        ada_steps = n_mod // ada_tn
        assert ada_steps <= n_i * n_j

        def ada_block(i, j):
            return jnp.minimum(i * n_j + j, ada_steps - 1)

        in_specs += [
            pl.BlockSpec((rows, d), lambda i, j: (0, 0)),
            pl.BlockSpec((None, d, ada_tn), lambda i, j: (layer + 1, 0, ada_block(i, j))),
            pl.BlockSpec((None, 1, ada_tn), lambda i, j: (layer + 1, 0, ada_block(i, j))),
        ]
        out_specs.append(pl.BlockSpec((rows, ada_tn), lambda i, j: (0, ada_block(i, j))))
        out_shape.append(jax.ShapeDtypeStruct((rows, n_mod), F32))
        args += [c_pad, w_ada, b_ada3]
    res = pl.pallas_call(
        functools.partial(_mlp_up_kernel, ada_steps=ada_steps),
        grid=(n_i, n_j),
        in_specs=in_specs,
        out_specs=out_specs,
        out_shape=out_shape,
        compiler_params=_params("arbitrary", "arbitrary") if ada_steps else _params("parallel", "parallel"),
        name="mlp_up",
    )(*args)
    return (res[0], res[1], res[2]) if ada_steps else (res[0], res[1], None)


def _out_proj_kernel(yc_ref, yf_ref, yr_ref, w_ref, o_ref):
    f0 = yc_ref.shape[1]
    r0 = f0 + yf_ref.shape[1]
    acc = jnp.dot(yc_ref[...], w_ref[0:f0, :].astype(BF16), preferred_element_type=F32)
    acc += jnp.dot(yf_ref[...], w_ref[f0:r0, :].astype(BF16), preferred_element_type=F32)
    acc += jnp.dot(yr_ref[...], w_ref[r0:, :].astype(BF16), preferred_element_type=F32)
    o_ref[...] = acc.astype(o_ref.dtype)


def _out_proj(y_conv, y_fox, y_ret, w, layer, tm=1024, tn=512):
    m = y_conv.shape[0]
    _, kdim, n = w.shape

    def lhs(width):
        return pl.BlockSpec((tm, width), lambda i, j: (i, 0))

    return pl.pallas_call(
        _out_proj_kernel,
        grid=(m // tm, n // tn),
        in_specs=[lhs(y_conv.shape[1]), lhs(y_fox.shape[1]), lhs(y_ret.shape[1]),
                  pl.BlockSpec((None, kdim, tn), lambda i, j: (layer, 0, j))],
        out_specs=pl.BlockSpec((tm, tn), lambda i, j: (i, j)),
        out_shape=jax.ShapeDtypeStruct((m, n), BF16),
        compiler_params=_params("parallel", "parallel"),
        name="out_proj",
    )(y_conv, y_fox, y_ret, w)


def _conv_kernel(u_ref, b_ref, c_ref, w_ref, o_ref, *, rows):
    seq, tc = u_ref.shape
    w0 = w_ref[0:1, :]
    w1 = w_ref[1:2, :]
    w2 = w_ref[2:3, :]
    rid = lax.broadcasted_iota(jnp.int32, (rows, 1), 0)

    def chunk(ci, carry):
        prev2, prev1 = carry
        r0 = pl.multiple_of(ci * rows, rows)
        z = c_ref[pl.ds(r0, rows), :].astype(F32) * u_ref[pl.ds(r0, rows), :].astype(F32)
        z1 = jnp.where(rid == 0, prev1, pltpu.roll(z, 1, 0))
        z2 = jnp.where(rid == 0, prev2, jnp.where(rid == 1, prev1, pltpu.roll(z, 2, 0)))
        y = w0 * z2 + w1 * z1 + w2 * z
        o_ref[pl.ds(r0, rows), :] = (b_ref[pl.ds(r0, rows), :].astype(F32) * y).astype(o_ref.dtype)
        return z[rows - 2:rows - 1, :], z[rows - 1:rows, :]

    zero = jnp.zeros((1, tc), F32)
    lax.fori_loop(0, seq // rows, chunk, (zero, zero))


def _short_conv(proj, conv_w, batch, seq, tc=256, rows=512):
    nblk = CONV_WIDTH // tc
    return pl.pallas_call(
        functools.partial(_conv_kernel, rows=rows),
        grid=(batch, nblk),
        in_specs=[
            pl.BlockSpec((seq, tc), lambda b, j: (b, j)),
            pl.BlockSpec((seq, tc), lambda b, j: (b, nblk + j)),
            pl.BlockSpec((seq, tc), lambda b, j: (b, 2 * nblk + j)),
            pl.BlockSpec((CONV_K, tc), lambda b, j: (0, j)),
        ],
        out_specs=pl.BlockSpec((seq, tc), lambda b, j: (b, j)),
        out_shape=jax.ShapeDtypeStruct((batch * seq, CONV_WIDTH), BF16),
        compiler_params=_params("parallel", "parallel"),
        name="short_conv",
    )(proj, proj, proj, conv_w)


def _cum_kernel(f_ref, bf_ref, o_ref, carry_ref):
    rows = f_ref.shape[0]

    @pl.when(pl.program_id(1) == 0)
    def _():
        carry_ref[...] = jnp.zeros_like(carry_ref)

    z = f_ref[...] + bf_ref[...]
    logf = jnp.minimum(z, 0.0) - jnp.log1p(jnp.exp(-jnp.abs(z)))
    p1 = logf.astype(BF16)
    r1 = logf - p1.astype(F32)
    p2 = r1.astype(BF16)
    p3 = (r1 - p2.astype(F32)).astype(BF16)
    ri = lax.broadcasted_iota(jnp.int32, (rows, rows), 0)
    ci = lax.broadcasted_iota(jnp.int32, (rows, rows), 1)
    tri = (ri >= ci).astype(BF16)
    cum = (jnp.dot(tri, p1, preferred_element_type=F32)
           + jnp.dot(tri, p2, preferred_element_type=F32)
           + jnp.dot(tri, p3, preferred_element_type=F32)) + carry_ref[...]
    o_ref[...] = cum.T
    carry_ref[...] = cum[rows - 1:rows, :]


def _forget_cumsum(f_logit, b_f_pad, batch, seq, rows):
    per_b = seq // rows
    return pl.pallas_call(
        _cum_kernel,
        grid=(batch, per_b),
        in_specs=[
            pl.BlockSpec((rows, LANE), lambda b, j: (b * per_b + j, 0)),
            pl.BlockSpec((1, LANE), lambda b, j: (0, 0)),
        ],
        out_specs=pl.BlockSpec((None, None, LANE, rows), lambda b, j: (b, j, 0, 0)),
        out_shape=jax.ShapeDtypeStruct((batch, per_b, LANE, rows), F32),
        scratch_shapes=[pltpu.VMEM((1, LANE), F32)],
        compiler_params=_params("parallel", "arbitrary"),
        name="forget_cumsum",
    )(f_logit, b_f_pad)


def _fox_kernel(q_ref, k_ref, v_ref, f_ref, o_ref, vt_ref, frep_ref, m_ref, acc_ref, *, tq, tk, hp):
    seq = q_ref.shape[0]
    inv_scale = HEAD_DIM ** 0.5
    exp2_scale = HEAD_DIM ** -0.5 * LOG2_E
    head0 = pl.program_id(1) * hp
    per_q = tq // tk
    rel_pos = (lax.broadcasted_iota(jnp.int32, (tk, tq), 1)
               - lax.broadcasted_iota(jnp.int32, (tk, tq), 0))

    def head_cols(hh):
        return slice(hh * HEAD_DIM, (hh + 1) * HEAD_DIM)

    for hh in range(hp):
        def prepare(j, carry, hh=hh):
            r0 = pl.multiple_of(j * tk, tk)
            vt_ref[hh, j, :HEAD_DIM] = v_ref[pl.ds(r0, tk), head_cols(hh)].astype(F32).T.astype(BF16)
            vt_ref[hh, j, HEAD_DIM:] = (
                lax.broadcasted_iota(jnp.int32, (ONES_ROWS, tk), 0) == 0).astype(BF16)
            f_row = f_ref[j, pl.ds(head0 + hh, 1), :] * inv_scale
            frep_ref[hh, j] = jnp.broadcast_to(f_row, (LANE, tk)).T
            return carry

        lax.fori_loop(0, seq // tk, prepare, 0)

    def q_block(qi, _):
        q0 = pl.multiple_of(qi * tq, tq)
        m_ref[...] = jnp.full_like(m_ref, -jnp.inf)
        acc_ref[...] = jnp.zeros_like(acc_ref)

        def kv_block(ki, diag):
            k0 = pl.multiple_of(ki * tk, tk)
            sub = tq // FOX_QSPLIT
            units = [(hh, c) for hh in range(hp) for c in range(FOX_QSPLIT)]
            scores = []
            for hh, c in units:
                q = q_ref[pl.ds(q0 + c * sub, sub), head_cols(hh)]
                k = k_ref[pl.ds(k0, tk), head_cols(hh)]
                scores.append(lax.dot_general(k, q, (((1,), (1,)), ((), ())),
                                              preferred_element_type=F32))
            for (hh, c), st in zip(units, scores):
                cols = slice(c * sub, (c + 1) * sub)
                u = st - jnp.concatenate([frep_ref[hh, ki]] * (sub // LANE), axis=1)
                if diag is not None:
                    u = jnp.where(rel_pos[:, cols] >= diag * tk, u, -jnp.inf)
                m_old = m_ref[hh, :, cols]
                m_new = jnp.maximum(m_old, jnp.max(u, axis=0, keepdims=True))
                p = jnp.exp2((u - m_new) * exp2_scale)
                alpha = jnp.exp2((m_old - m_new) * exp2_scale)
                acc_ref[hh, :, cols] = alpha * acc_ref[hh, :, cols] + jnp.dot(
                    vt_ref[hh, ki], p.astype(BF16), preferred_element_type=F32)
                m_ref[hh, :, cols] = m_new

        def full_block(ki, carry):
            kv_block(ki, None)
            return carry

        lax.fori_loop(0, qi * per_q, full_block, 0)
        for d in range(per_q):
            kv_block(qi * per_q + d, d)
        for hh in range(hp):
            out_t = acc_ref[hh, :HEAD_DIM] / acc_ref[hh, HEAD_DIM:HEAD_DIM + 1]
            o_ref[pl.ds(q0, tq), head_cols(hh)] = out_t.T.astype(o_ref.dtype)
        return 0

    lax.fori_loop(0, seq // tq, q_block, 0)


def _forgetting_attention(proj, cum, batch, seq, tq, tk, hp):
    groups = FOX_HEADS // hp
    width = hp * HEAD_DIM
    qoff = FOX_OFF // width
    fox_blocks = FOX_WIDTH // width
    head_rows = 16

    def head_spec(off):
        return pl.BlockSpec((seq, width), lambda b, g: (b, off + g))

    return pl.pallas_call(
        functools.partial(_fox_kernel, tq=tq, tk=tk, hp=hp),
        grid=(batch, groups),
        in_specs=[
            head_spec(qoff), head_spec(qoff + fox_blocks), head_spec(qoff + 2 * fox_blocks),
            pl.BlockSpec((None, seq // tk, head_rows, tk), lambda b, g: (b, 0, 0, 0)),
        ],
        out_specs=pl.BlockSpec((seq, width), lambda b, g: (b, g)),
        out_shape=jax.ShapeDtypeStruct((batch * seq, FOX_WIDTH), BF16),
        scratch_shapes=[
            pltpu.VMEM((hp, seq // tk, HEAD_DIM + ONES_ROWS, tk), BF16),
            pltpu.VMEM((hp, seq // tk, tk, LANE), F32),
            pltpu.VMEM((hp, 1, tq), F32),
            pltpu.VMEM((hp, HEAD_DIM + ONES_ROWS, tq), F32),
        ],
        compiler_params=_params("parallel", "parallel"),
        name="forgetting_attention",
    )(proj, proj, proj, cum)


def _ret_kernel(lg_ref, q_ref, k_ref, v_ref, g_ref, cos_ref, sin_ref, o_ref, *, chunk, hp):
    seq = q_ref.shape[0]
    scale = HEAD_DIM ** -0.5
    half = HEAD_DIM // 2
    ri = lax.broadcasted_iota(jnp.int32, (chunk, chunk), 0)
    ci = lax.broadcasted_iota(jnp.int32, (chunk, chunk), 1)
    diff = (ri - ci).astype(F32)
    jv = lax.broadcasted_iota(jnp.int32, (chunk, 1), 0).astype(F32)
    decays = []
    for hh in range(hp):
        lg = lg_ref[pl.program_id(1) * hp + hh]
        decays.append((
            jnp.where(diff >= 0.0, jnp.exp(jnp.maximum(diff, 0.0) * lg), 0.0),
            jnp.exp((jv + 1.0) * lg),
            jnp.exp((chunk - 1.0 - jv) * lg),
            jnp.exp(jnp.full((1, 1), float(chunk), F32) * lg),
        ))

    def rotate(t, cosf, sinf):
        return t * cosf + pltpu.roll(t, half, 1) * sinf

    def step(c, states):
        r0 = pl.multiple_of(c * chunk, chunk)
        rows = pl.ds(r0, chunk)
        cosf = cos_ref[rows, :]
        sinf = sin_ref[rows, :]
        new_states = []
        staged = []
        for hh in range(hp):
            cols = slice(hh * HEAD_DIM, (hh + 1) * HEAD_DIM)
            q = rotate(q_ref[rows, cols].astype(F32), cosf, sinf)
            k = rotate(k_ref[rows, cols].astype(F32), cosf, sinf) * scale
            scores = lax.dot_general(q.astype(BF16), k.astype(BF16), (((1,), (1,)), ((), ())),
                                     preferred_element_type=F32)
            staged.append((q, k, scores))
        for hh in range(hp):
            cols = slice(hh * HEAD_DIM, (hh + 1) * HEAD_DIM)
            intra, q_decay, k_decay, chunk_decay = decays[hh]
            state = states[hh]
            q, k, scores = staged[hh]
            v = v_ref[rows, cols]
            out = (jnp.dot((scores * intra).astype(BF16), v, preferred_element_type=F32)
                   + jnp.dot((q * q_decay).astype(BF16), state.astype(BF16), preferred_element_type=F32))
            new_states.append(state * chunk_decay + lax.dot_general(
                (k * k_decay).astype(BF16), v, (((0,), (0,)), ((), ())), preferred_element_type=F32))
            mu = jnp.mean(out, axis=1, keepdims=True)
            cen = out - mu
            var = jnp.mean(cen * cen, axis=1, keepdims=True)
            y = cen * lax.rsqrt(var + GN_EPS)
            g = g_ref[rows, cols].astype(F32)
            o_ref[rows, cols] = (y * (g / (1.0 + jnp.exp(-g)))).astype(o_ref.dtype)
        return tuple(new_states)

    zero = jnp.zeros((HEAD_DIM, HEAD_DIM), F32)
    lax.fori_loop(0, seq // chunk, step, (zero,) * hp)


def _retention(proj, log_gamma, cosf, sinf, batch, seq, chunk=256, hp=4):
    groups = RET_HEADS // hp
    width = hp * HEAD_DIM
    qoff = RET_MAIN_OFF // width
    ret_blocks = RET_WIDTH // width

    def head_spec(off):
        return pl.BlockSpec((seq, width), lambda b, g, lg: (b, off + g))

    table = pl.BlockSpec((seq, HEAD_DIM), lambda b, g, lg: (0, 0))
    return pl.pallas_call(
        functools.partial(_ret_kernel, chunk=chunk, hp=hp),
        grid_spec=pltpu.PrefetchScalarGridSpec(
            num_scalar_prefetch=1,
            grid=(batch, groups),
            in_specs=[head_spec(qoff), head_spec(qoff + ret_blocks), head_spec(qoff + 2 * ret_blocks),
                      head_spec(qoff + 3 * ret_blocks), table, table],
            out_specs=pl.BlockSpec((seq, width), lambda b, g, lg: (b, g)),
        ),
        out_shape=jax.ShapeDtypeStruct((batch * seq, RET_WIDTH), BF16),
        compiler_params=_params("parallel", "parallel"),
        name="retention",
    )(log_gamma, proj, proj, proj, proj, cosf, sinf)


def _ln_kernel(x_ref, y_ref, g_ref, gain_ref, bias_ref, *rest, alpha, modulate):
    r = alpha * x_ref[...] + g_ref[...] * y_ref[...].astype(F32)
    mu = jnp.mean(r, axis=1, keepdims=True)
    cen = r - mu
    var = jnp.mean(cen * cen, axis=1, keepdims=True)
    xn = cen * lax.rsqrt(var + LN_EPS) * gain_ref[...] + bias_ref[...]
    if modulate:
        sc_ref, sh_ref, xo_ref, h_ref = rest
        xo_ref[...] = xn
        h_ref[...] = (xn * (1.0 + sc_ref[...]) + sh_ref[...]).astype(h_ref.dtype)
    else:
        xo_ref, = rest
        xo_ref[...] = xn


def _residual_ln(x2d, branch, gate, gain, bias, sc, sh, seq, alpha, tm=256):
    m, d = x2d.shape
    per_b = seq // tm
    modulate = sc is not None
    row = pl.BlockSpec((tm, d), lambda i: (i, 0))
    vec_b = pl.BlockSpec((None, 1, d), lambda i: (i // per_b, 0, 0))
    vec = pl.BlockSpec((1, d), lambda i: (0, 0))
    in_specs = [row, row, vec_b, vec, vec]
    args = [x2d, branch, gate, gain, bias]
    out_specs = [row]
    out_shape = [jax.ShapeDtypeStruct((m, d), F32)]
    if modulate:
        in_specs += [vec_b, vec_b]
        args += [sc, sh]
        out_specs.append(row)
        out_shape.append(jax.ShapeDtypeStruct((m, d), BF16))
    res = pl.pallas_call(
        functools.partial(_ln_kernel, alpha=alpha, modulate=modulate),
        grid=(m // tm,),
        in_specs=in_specs,
        out_specs=out_specs,
        out_shape=out_shape,
        compiler_params=_params("parallel"),
        name="residual_layernorm",
    )(*args)
    return (res[0], res[1]) if modulate else (res[0], None)


def _rotary_tables(seq):
    pos = np.arange(seq, dtype=np.float32)
    inv_freq = (np.float32(ROPE_BASE) ** (-np.arange(0, HEAD_DIM, 2, dtype=np.float32) / np.float32(HEAD_DIM))
                ).astype(np.float32)
    ang = (pos[:, None] * inv_freq[None, :]).astype(np.float32)
    cos = np.cos(ang).astype(np.float32)
    sin = np.sin(ang).astype(np.float32)
    return (jnp.asarray(np.concatenate([cos, cos], axis=1)),
            jnp.asarray(np.concatenate([-sin, sin], axis=1)))


def _log_gamma():
    h = np.arange(RET_HEADS, dtype=np.float32)
    return jnp.asarray(np.log1p(-np.exp2(np.float32(-5.0) - h)).astype(np.float32))


def kernel(x, c, w_ada, b_ada, w_in, b_f, conv_w, w_out, ln1_g, ln1_b, w_up, w_down, ln2_g, ln2_b):
    batch, seq, d = x.shape
    depth = w_ada.shape[0]
    m = batch * seq
    alpha = (2.0 * depth) ** 0.25
    pad_rows = 8

    c_pad = jnp.zeros((pad_rows, d), F32).at[:batch].set(c)
    b_ada3 = b_ada.reshape(depth, 1, 6 * d)

    def split_mod(mod_rows):
        return mod_rows[:batch].reshape(batch, 6, 1, d)

    mod = split_mod(_ada_first_layer(c_pad, w_ada, b_ada3))

    cosf, sinf = _rotary_tables(seq)
    log_gamma = _log_gamma()

    w_main_t, w_f_t = _prep_in_proj_weights(w_in)
    b_f_pad = jnp.zeros((depth, 1, LANE), F32).at[:, 0, :FOX_HEADS].set(b_f)

    x2d = x.reshape(m, d)
    h = _modulate(x2d, mod[:, 1], mod[:, 0], seq)
    for l in range(depth):
        sh_m, sc_m, g_a, g_m = mod[:, 3], mod[:, 4], mod[:, 2], mod[:, 5]
        last = l + 1 == depth

        proj = _matmul_nt(h, w_main_t, l, tm=1024, tn=1536, out_dtype=BF16, name="in_proj")
        f_logit = _matmul_nt(h, w_f_t, l, tm=1024, tn=LANE, out_dtype=F32, name="forget_logits")

        cum = _forget_cumsum(f_logit, b_f_pad[l], batch, seq, rows=FOX_TK)
        y_conv = _short_conv(proj, conv_w[l], batch, seq)
        y_fox = _forgetting_attention(proj, cum, batch, seq, tq=FOX_TQ, tk=FOX_TK, hp=FOX_HP)
        y_ret = _retention(proj, log_gamma, cosf, sinf, batch, seq)

        mix = _out_proj(y_conv, y_fox, y_ret, w_out, l)
        x2d, h2 = _residual_ln(x2d, mix, g_a, ln1_g[l][None], ln1_b[l][None], sc_m, sh_m, seq, alpha)

        a, w_down_l, mod_next = _mlp_up(h2, w_up, w_down, l,
                                        ada_next=None if last else (c_pad, w_ada, b_ada3))
        mlp = _matmul_ksplit(a, w_down_l, tm=1024, tn=1024, tk=d, out_dtype=BF16, name="mlp_down")
        if last:
            x2d, _ = _residual_ln(x2d, mlp, g_m, ln2_g[l][None], ln2_b[l][None], None, None, seq, alpha)
        else:
            mod = split_mod(mod_next)
            x2d, h = _residual_ln(x2d, mlp, g_m, ln2_g[l][None], ln2_b[l][None],
                                  mod[:, 1], mod[:, 0], seq, alpha)
    return x2d.reshape(batch, seq, d)
```

```python
import functools

import numpy as np
import jax
import jax.numpy as jnp
from jax import lax
from jax.experimental import pallas as pl
from jax.experimental.pallas import tpu as pltpu

D_MODEL = 4096
HEAD_DIM = 128
CONV_WIDTH = D_MODEL // 4
FOX_WIDTH = (D_MODEL - CONV_WIDTH) // 2
RET_WIDTH = D_MODEL - CONV_WIDTH - FOX_WIDTH
FOX_HEADS = FOX_WIDTH // HEAD_DIM
RET_HEADS = RET_WIDTH // HEAD_DIM
CONV_K = 3
ROPE_BASE = 10000.0
LN_EPS = 1e-5
GN_EPS = 1e-6

FOX_OFF = 3 * CONV_WIDTH
FLOGIT_OFF = FOX_OFF + 3 * FOX_WIDTH
RET_OFF = FLOGIT_OFF + FOX_HEADS
MAIN_WIDTH = 3 * CONV_WIDTH + 3 * FOX_WIDTH + 4 * RET_WIDTH
RET_MAIN_OFF = FOX_OFF + 3 * FOX_WIDTH

LANE = 128
LOG2_E = 1.4426950408889634

FOX_TQ = 1024
FOX_TK = 512
FOX_HP = 2
FOX_QSPLIT = 2
ONES_ROWS = 16
VMEM_LIMIT = 56 * 1024 * 1024
SIDE_JOB_VMEM_LIMIT = 60 * 1024 * 1024

BF16 = jnp.bfloat16
F32 = jnp.float32


def _params(*sem, vmem_limit=VMEM_LIMIT):
    return pltpu.CompilerParams(dimension_semantics=sem, vmem_limit_bytes=vmem_limit)


def _ada_block(c_ref, w_ref, b_ref):
    c = c_ref[...]
    cond = (c / (1.0 + jnp.exp(-c))).astype(BF16)
    return jnp.dot(cond, w_ref[...].astype(BF16), preferred_element_type=F32) + b_ref[...]


def _ada_kernel(c_ref, w_ref, b_ref, o_ref):
    o_ref[...] = _ada_block(c_ref, w_ref, b_ref)


def _ada_first_layer(c_pad, w_ada, b_ada3, tn=512):
    _, d, n = w_ada.shape
    rows = c_pad.shape[0]
    return pl.pallas_call(
        _ada_kernel,
        grid=(n // tn,),
        in_specs=[
            pl.BlockSpec((rows, d), lambda j: (0, 0)),
            pl.BlockSpec((None, d, tn), lambda j: (0, 0, j)),
            pl.BlockSpec((None, 1, tn), lambda j: (0, 0, j)),
        ],
        out_specs=pl.BlockSpec((rows, tn), lambda j: (0, j)),
        out_shape=jax.ShapeDtypeStruct((rows, n), F32),
        compiler_params=_params("parallel"),
        name="ada_modulation",
    )(c_pad, w_ada, b_ada3)


def _modulate_kernel(x_ref, sc_ref, sh_ref, h_ref):
    h_ref[...] = (x_ref[...] * (1.0 + sc_ref[...]) + sh_ref[...]).astype(h_ref.dtype)


def _modulate(x2d, sc, sh, seq, tm=512):
    m, d = x2d.shape
    per_b = seq // tm
    vec = pl.BlockSpec((None, 1, d), lambda i: (i // per_b, 0, 0))
    return pl.pallas_call(
        _modulate_kernel,
        grid=(m // tm,),
        in_specs=[pl.BlockSpec((tm, d), lambda i: (i, 0)), vec, vec],
        out_specs=pl.BlockSpec((tm, d), lambda i: (i, 0)),
        out_shape=jax.ShapeDtypeStruct((m, d), BF16),
        compiler_params=_params("parallel"),
        name="modulate",
    )(x2d, sc, sh)


def _prep_main_kernel(a_ref, b_ref, o_ref, *, first_shifted, shift):
    @pl.when(pl.program_id(0) < first_shifted)
    def _():
        o_ref[...] = jnp.swapaxes(a_ref[...], 0, 1).astype(o_ref.dtype)

    @pl.when(pl.program_id(0) >= first_shifted)
    def _():
        rows = jnp.concatenate([a_ref[shift:, :, :], b_ref[:shift, :, :]], axis=0)
        o_ref[...] = jnp.swapaxes(rows, 0, 1).astype(o_ref.dtype)


def _prep_flogit_kernel(a_ref, o_ref, *, count):
    tail = a_ref.shape[0]
    v = jnp.swapaxes(a_ref[...], 0, 1)
    keep = lax.broadcasted_iota(jnp.int32, v.shape, 1) < count
    o_ref[...] = jnp.zeros_like(o_ref)
    o_ref[:, :tail, :] = jnp.where(keep, v, 0.0).astype(o_ref.dtype)


def _prep_in_proj_weights(w_in, rows=256, tail=16):
    depth, d, _ = w_in.shape
    w_nlk = jnp.transpose(w_in, (2, 0, 1))
    shift = RET_OFF - FLOGIT_OFF
    w_main_t = pl.pallas_call(
        functools.partial(_prep_main_kernel, first_shifted=FLOGIT_OFF // rows, shift=shift),
        grid=(MAIN_WIDTH // rows,),
        in_specs=[
            pl.BlockSpec((rows, depth, d), lambda i: (i, 0, 0)),
            pl.BlockSpec((tail, depth, d), lambda i: ((i + 1) * (rows // tail), 0, 0)),
        ],
        out_specs=pl.BlockSpec((depth, rows, d), lambda i: (0, i, 0)),
        out_shape=jax.ShapeDtypeStruct((depth, MAIN_WIDTH, d), BF16),
        compiler_params=_params("parallel"),
        name="prep_in_proj_weights",
    )(w_nlk, w_nlk)
    w_f_t = pl.pallas_call(
        functools.partial(_prep_flogit_kernel, count=FOX_HEADS),
        grid=(1,),
        in_specs=[pl.BlockSpec((tail, depth, d), lambda i: (FLOGIT_OFF // tail, 0, 0))],
        out_specs=pl.BlockSpec((depth, LANE, d), lambda i: (0, 0, 0)),
        out_shape=jax.ShapeDtypeStruct((depth, LANE, d), BF16),
        compiler_params=_params("arbitrary"),
        name="prep_forget_logit_weights",
    )(w_nlk)
    return w_main_t, w_f_t


def _mm_nt_kernel(a_ref, wt_ref, o_ref):
    acc = lax.dot_general(a_ref[...], wt_ref[...], (((1,), (1,)), ((), ())),
                          preferred_element_type=F32)
    o_ref[...] = acc.astype(o_ref.dtype)


def _matmul_nt(a, wt, layer, *, tm, tn, out_dtype, name):
    m, kdim = a.shape
    n = wt.shape[1]
    return pl.pallas_call(
        _mm_nt_kernel,
        grid=(m // tm, n // tn),
        in_specs=[
            pl.BlockSpec((tm, kdim), lambda i, j: (i, 0)),
            pl.BlockSpec((None, tn, kdim), lambda i, j: (layer, j, 0)),
        ],
        out_specs=pl.BlockSpec((tm, tn), lambda i, j: (i, j)),
        out_shape=jax.ShapeDtypeStruct((m, n), out_dtype),
        compiler_params=_params("parallel", "parallel"),
        name=name,
    )(a, wt)


def _in_proj_kernel(h_ref, wt_ref, wu_ref, o_ref, wu_bf16_ref):
    acc = lax.dot_general(h_ref[...], wt_ref[...], (((1,), (1,)), ((), ())),
                          preferred_element_type=F32)
    o_ref[...] = acc.astype(o_ref.dtype)
    wu_bf16_ref[...] = wu_ref[...].astype(wu_bf16_ref.dtype)


def _in_proj(h, wt, w_up, layer, tm=1024, tn=1536):
    m, d = h.shape
    n = wt.shape[1]
    d_ff = w_up.shape[-1]
    n_i, n_j = m // tm, n // tn
    cast_j = n_j - 1
    assert d % (n_i * cast_j) == 0
    slab = d // (n_i * cast_j)

    def slab_index(i, j):
        return i * cast_j + jnp.minimum(j, cast_j - 1)

    return pl.pallas_call(
        _in_proj_kernel,
        grid=(n_i, n_j),
        in_specs=[
            pl.BlockSpec((tm, d), lambda i, j: (i, 0), pipeline_mode=pl.Buffered(1)),
            pl.BlockSpec((None, tn, d), lambda i, j: (layer, j, 0)),
            pl.BlockSpec((None, slab, d_ff), lambda i, j: (layer, slab_index(i, j), 0)),
        ],
        out_specs=[
            pl.BlockSpec((tm, tn), lambda i, j: (i, j)),
            pl.BlockSpec((slab, d_ff), lambda i, j: (slab_index(i, j), 0)),
        ],
        out_shape=[jax.ShapeDtypeStruct((m, n), BF16), jax.ShapeDtypeStruct((d, d_ff), BF16)],
        compiler_params=_params("arbitrary", "arbitrary", vmem_limit=SIDE_JOB_VMEM_LIMIT),
        name="in_proj",
    )(h, wt, w_up)


def _mm_ksplit_kernel(a_ref, w_ref, o_ref, acc_ref, *, nk):
    part = jnp.dot(a_ref[...], w_ref[...], preferred_element_type=F32)
    k = pl.program_id(2)

    @pl.when(k == 0)
    def _():
        acc_ref[...] = part

    @pl.when(jnp.logical_and(k > 0, k < nk - 1))
    def _():
        acc_ref[...] += part

    @pl.when(k == nk - 1)
    def _():
        o_ref[...] = (acc_ref[...] + part).astype(o_ref.dtype)


def _matmul_ksplit(a, w, *, tm, tn, tk, out_dtype, name):
    m, kdim = a.shape
    n = w.shape[1]
    nk = kdim // tk
    assert nk >= 2
    return pl.pallas_call(
        functools.partial(_mm_ksplit_kernel, nk=nk),
        grid=(m // tm, n // tn, nk),
        in_specs=[
            pl.BlockSpec((tm, tk), lambda i, j, k: (i, k)),
            pl.BlockSpec((tk, tn), lambda i, j, k: (k, j)),
        ],
        out_specs=pl.BlockSpec((tm, tn), lambda i, j, k: (i, j)),
        out_shape=jax.ShapeDtypeStruct((m, n), out_dtype),
        scratch_shapes=[pltpu.VMEM((tm, tn), F32)],
        compiler_params=_params("parallel", "parallel", "arbitrary"),
        name=name,
    )(a, w)


def _mlp_up_kernel(h_ref, w_ref, wd_ref, *rest, ada_steps):
    if ada_steps:
        c_ref, wa_ref, ba_ref, a_ref, wd_bf16_ref, mod_ref = rest
    else:
        a_ref, wd_bf16_ref = rest
    acc = jnp.dot(h_ref[...], w_ref[...], preferred_element_type=F32)
    a_ref[...] = jnp.square(jnp.maximum(acc, 0.0)).astype(a_ref.dtype)
    wd_bf16_ref[...] = wd_ref[...].astype(wd_bf16_ref.dtype)
    if ada_steps:
        step = pl.program_id(0) * pl.num_programs(1) + pl.program_id(1)

        @pl.when(step < ada_steps)
        def _():
            mod_ref[...] = _ada_block(c_ref, wa_ref, ba_ref)


def _mlp_up(h, w_up_bf16, w_down, layer, ada_next=None, tm=1024, tn=1024, ada_tn=256):
    m, d = h.shape
    d_ff = w_up_bf16.shape[-1]
    n_i, n_j = m // tm, d_ff // tn
    slab = d_ff // (n_i * n_j)
    in_specs = [
        pl.BlockSpec((tm, d), lambda i, j: (i, 0)),
        pl.BlockSpec((d, tn), lambda i, j: (0, j)),
        pl.BlockSpec((None, slab, d), lambda i, j: (layer, i * n_j + j, 0)),
    ]
    out_specs = [
        pl.BlockSpec((tm, tn), lambda i, j: (i, j)),
        pl.BlockSpec((slab, d), lambda i, j: (i * n_j + j, 0)),
    ]
    out_shape = [jax.ShapeDtypeStruct((m, d_ff), BF16), jax.ShapeDtypeStruct((d_ff, d), BF16)]
    args = [h, w_up_bf16, w_down]
    ada_steps = 0
    if ada_next is not None:
        c_pad, w_ada, b_ada3 = ada_next
        rows, n_mod = c_pad.shape[0], w_ada.shape[-1]
        ada_steps = n_mod // ada_tn
        assert ada_steps <= n_i * n_j

        def ada_block(i, j):
            return jnp.minimum(i * n_j + j, ada_steps - 1)

        in_specs += [
            pl.BlockSpec((rows, d), lambda i, j: (0, 0)),
            pl.BlockSpec((None, d, ada_tn), lambda i, j: (layer + 1, 0, ada_block(i, j))),
            pl.BlockSpec((None, 1, ada_tn), lambda i, j: (layer + 1, 0, ada_block(i, j))),
        ]
        out_specs.append(pl.BlockSpec((rows, ada_tn), lambda i, j: (0, ada_block(i, j))))
        out_shape.append(jax.ShapeDtypeStruct((rows, n_mod), F32))
        args += [c_pad, w_ada, b_ada3]
    res = pl.pallas_call(
        functools.partial(_mlp_up_kernel, ada_steps=ada_steps),
        grid=(n_i, n_j),
        in_specs=in_specs,
        out_specs=out_specs,
        out_shape=out_shape,
        compiler_params=(_params("arbitrary", "arbitrary", vmem_limit=SIDE_JOB_VMEM_LIMIT) if ada_steps
                         else _params("parallel", "parallel")),
        name="mlp_up",
    )(*args)
    return (res[0], res[1], res[2]) if ada_steps else (res[0], res[1], None)


def _out_proj_kernel(yc_ref, yf_ref, yr_ref, w_ref, o_ref):
    f0 = yc_ref.shape[1]
    r0 = f0 + yf_ref.shape[1]
    acc = jnp.dot(yc_ref[...], w_ref[0:f0, :].astype(BF16), preferred_element_type=F32)
    acc += jnp.dot(yf_ref[...], w_ref[f0:r0, :].astype(BF16), preferred_element_type=F32)
    acc += jnp.dot(yr_ref[...], w_ref[r0:, :].astype(BF16), preferred_element_type=F32)
    o_ref[...] = acc.astype(o_ref.dtype)


def _out_proj(y_conv, y_fox, y_ret, w, layer, tm=1024, tn=512):
    m = y_conv.shape[0]
    _, kdim, n = w.shape

    def lhs(width):
        return pl.BlockSpec((tm, width), lambda i, j: (i, 0))

    return pl.pallas_call(
        _out_proj_kernel,
        grid=(m // tm, n // tn),
        in_specs=[lhs(y_conv.shape[1]), lhs(y_fox.shape[1]), lhs(y_ret.shape[1]),
                  pl.BlockSpec((None, kdim, tn), lambda i, j: (layer, 0, j))],
        out_specs=pl.BlockSpec((tm, tn), lambda i, j: (i, j)),
        out_shape=jax.ShapeDtypeStruct((m, n), BF16),
        compiler_params=_params("parallel", "parallel"),
        name="out_proj",
    )(y_conv, y_fox, y_ret, w)


def _conv_kernel(u_ref, b_ref, c_ref, w_ref, o_ref, *, rows):
    seq, tc = u_ref.shape
    w0 = w_ref[0:1, :]
    w1 = w_ref[1:2, :]
    w2 = w_ref[2:3, :]
    rid = lax.broadcasted_iota(jnp.int32, (rows, 1), 0)

    def chunk(ci, carry):
        prev2, prev1 = carry
        r0 = pl.multiple_of(ci * rows, rows)
        z = c_ref[pl.ds(r0, rows), :].astype(F32) * u_ref[pl.ds(r0, rows), :].astype(F32)
        z1 = jnp.where(rid == 0, prev1, pltpu.roll(z, 1, 0))
        z2 = jnp.where(rid == 0, prev2, jnp.where(rid == 1, prev1, pltpu.roll(z, 2, 0)))
        y = w0 * z2 + w1 * z1 + w2 * z
        o_ref[pl.ds(r0, rows), :] = (b_ref[pl.ds(r0, rows), :].astype(F32) * y).astype(o_ref.dtype)
        return z[rows - 2:rows - 1, :], z[rows - 1:rows, :]

    zero = jnp.zeros((1, tc), F32)
    lax.fori_loop(0, seq // rows, chunk, (zero, zero))


def _short_conv(proj, conv_w, batch, seq, tc=256, rows=512):
    nblk = CONV_WIDTH // tc
    return pl.pallas_call(
        functools.partial(_conv_kernel, rows=rows),
        grid=(batch, nblk),
        in_specs=[
            pl.BlockSpec((seq, tc), lambda b, j: (b, j)),
            pl.BlockSpec((seq, tc), lambda b, j: (b, nblk + j)),
            pl.BlockSpec((seq, tc), lambda b, j: (b, 2 * nblk + j)),
            pl.BlockSpec((CONV_K, tc), lambda b, j: (0, j)),
        ],
        out_specs=pl.BlockSpec((seq, tc), lambda b, j: (b, j)),
        out_shape=jax.ShapeDtypeStruct((batch * seq, CONV_WIDTH), BF16),
        compiler_params=_params("parallel", "parallel"),
        name="short_conv",
    )(proj, proj, proj, conv_w)


def _cum_kernel(f_ref, bf_ref, o_ref, carry_ref):
    rows = f_ref.shape[0]

    @pl.when(pl.program_id(1) == 0)
    def _():
        carry_ref[...] = jnp.zeros_like(carry_ref)

    z = f_ref[...] + bf_ref[...]
    logf = jnp.minimum(z, 0.0) - jnp.log1p(jnp.exp(-jnp.abs(z)))
    p1 = logf.astype(BF16)
    r1 = logf - p1.astype(F32)
    p2 = r1.astype(BF16)
    p3 = (r1 - p2.astype(F32)).astype(BF16)
    ri = lax.broadcasted_iota(jnp.int32, (rows, rows), 0)
    ci = lax.broadcasted_iota(jnp.int32, (rows, rows), 1)
    tri = (ri >= ci).astype(BF16)
    cum = (jnp.dot(tri, p1, preferred_element_type=F32)
           + jnp.dot(tri, p2, preferred_element_type=F32)
           + jnp.dot(tri, p3, preferred_element_type=F32)) + carry_ref[...]
    o_ref[...] = cum.T
    carry_ref[...] = cum[rows - 1:rows, :]


def _forget_cumsum(f_logit, b_f_pad, batch, seq, rows):
    per_b = seq // rows
    return pl.pallas_call(
        _cum_kernel,
        grid=(batch, per_b),
        in_specs=[
            pl.BlockSpec((rows, LANE), lambda b, j: (b * per_b + j, 0)),
            pl.BlockSpec((1, LANE), lambda b, j: (0, 0)),
        ],
        out_specs=pl.BlockSpec((None, None, LANE, rows), lambda b, j: (b, j, 0, 0)),
        out_shape=jax.ShapeDtypeStruct((batch, per_b, LANE, rows), F32),
        scratch_shapes=[pltpu.VMEM((1, LANE), F32)],
        compiler_params=_params("parallel", "arbitrary"),
        name="forget_cumsum",
    )(f_logit, b_f_pad)


def _fox_kernel(q_ref, k_ref, v_ref, f_ref, o_ref, vt_ref, frep_ref, m_ref, acc_ref, *, tq, tk, hp):
    seq = q_ref.shape[0]
    inv_scale = HEAD_DIM ** 0.5
    exp2_scale = HEAD_DIM ** -0.5 * LOG2_E
    head0 = pl.program_id(1) * hp
    per_q = tq // tk
    rel_pos = (lax.broadcasted_iota(jnp.int32, (tk, tq), 1)
               - lax.broadcasted_iota(jnp.int32, (tk, tq), 0))

    def head_cols(hh):
        return slice(hh * HEAD_DIM, (hh + 1) * HEAD_DIM)

    for hh in range(hp):
        def prepare(j, carry, hh=hh):
            r0 = pl.multiple_of(j * tk, tk)
            vt_ref[hh, j, :HEAD_DIM] = v_ref[pl.ds(r0, tk), head_cols(hh)].astype(F32).T.astype(BF16)
            vt_ref[hh, j, HEAD_DIM:] = (
                lax.broadcasted_iota(jnp.int32, (ONES_ROWS, tk), 0) == 0).astype(BF16)
            f_row = f_ref[j, pl.ds(head0 + hh, 1), :] * inv_scale
            frep_ref[hh, j] = jnp.broadcast_to(f_row, (LANE, tk)).T
            return carry

        lax.fori_loop(0, seq // tk, prepare, 0)

    def q_block(qi, _):
        q0 = pl.multiple_of(qi * tq, tq)
        m_ref[...] = jnp.full_like(m_ref, -jnp.inf)
        acc_ref[...] = jnp.zeros_like(acc_ref)

        def kv_block(ki, diag):
            k0 = pl.multiple_of(ki * tk, tk)
            sub = tq // FOX_QSPLIT
            units = [(hh, c) for hh in range(hp) for c in range(FOX_QSPLIT)]
            scores = []
            for hh, c in units:
                q = q_ref[pl.ds(q0 + c * sub, sub), head_cols(hh)]
                k = k_ref[pl.ds(k0, tk), head_cols(hh)]
                scores.append(lax.dot_general(k, q, (((1,), (1,)), ((), ())),
                                              preferred_element_type=F32))
            for (hh, c), st in zip(units, scores):
                cols = slice(c * sub, (c + 1) * sub)
                u = st - jnp.concatenate([frep_ref[hh, ki]] * (sub // LANE), axis=1)
                if diag is not None:
                    u = jnp.where(rel_pos[:, cols] >= diag * tk, u, -jnp.inf)
                m_old = m_ref[hh, :, cols]
                m_new = jnp.maximum(m_old, jnp.max(u, axis=0, keepdims=True))
                p = jnp.exp2((u - m_new) * exp2_scale)
                alpha = jnp.exp2((m_old - m_new) * exp2_scale)
                acc_ref[hh, :, cols] = alpha * acc_ref[hh, :, cols] + jnp.dot(
                    vt_ref[hh, ki], p.astype(BF16), preferred_element_type=F32)
                m_ref[hh, :, cols] = m_new

        def full_block(ki, carry):
            kv_block(ki, None)
            return carry

        lax.fori_loop(0, qi * per_q, full_block, 0)
        for d in range(per_q):
            kv_block(qi * per_q + d, d)
        for hh in range(hp):
            out_t = acc_ref[hh, :HEAD_DIM] / acc_ref[hh, HEAD_DIM:HEAD_DIM + 1]
            o_ref[pl.ds(q0, tq), head_cols(hh)] = out_t.T.astype(o_ref.dtype)
        return 0

    lax.fori_loop(0, seq // tq, q_block, 0)


def _forgetting_attention(proj, cum, batch, seq, tq, tk, hp):
    groups = FOX_HEADS // hp
    width = hp * HEAD_DIM
    qoff = FOX_OFF // width
    fox_blocks = FOX_WIDTH // width
    head_rows = 16

    def head_spec(off):
        return pl.BlockSpec((seq, width), lambda b, g: (b, off + g))

    return pl.pallas_call(
        functools.partial(_fox_kernel, tq=tq, tk=tk, hp=hp),
        grid=(batch, groups),
        in_specs=[
            head_spec(qoff), head_spec(qoff + fox_blocks), head_spec(qoff + 2 * fox_blocks),
            pl.BlockSpec((None, seq // tk, head_rows, tk), lambda b, g: (b, 0, 0, 0)),
        ],
        out_specs=pl.BlockSpec((seq, width), lambda b, g: (b, g)),
        out_shape=jax.ShapeDtypeStruct((batch * seq, FOX_WIDTH), BF16),
        scratch_shapes=[
            pltpu.VMEM((hp, seq // tk, HEAD_DIM + ONES_ROWS, tk), BF16),
            pltpu.VMEM((hp, seq // tk, tk, LANE), F32),
            pltpu.VMEM((hp, 1, tq), F32),
            pltpu.VMEM((hp, HEAD_DIM + ONES_ROWS, tq), F32),
        ],
        compiler_params=_params("parallel", "parallel"),
        name="forgetting_attention",
    )(proj, proj, proj, cum)


def _ret_kernel(lg_ref, q_ref, k_ref, v_ref, g_ref, cos_ref, sin_ref, o_ref, *, chunk, hp):
    seq = q_ref.shape[0]
    scale = HEAD_DIM ** -0.5
    half = HEAD_DIM // 2
    ri = lax.broadcasted_iota(jnp.int32, (chunk, chunk), 0)
    ci = lax.broadcasted_iota(jnp.int32, (chunk, chunk), 1)
    diff = (ri - ci).astype(F32)
    jv = lax.broadcasted_iota(jnp.int32, (chunk, 1), 0).astype(F32)
    decays = []
    for hh in range(hp):
        lg = lg_ref[pl.program_id(1) * hp + hh]
        decays.append((
            jnp.where(diff >= 0.0, jnp.exp(jnp.maximum(diff, 0.0) * lg), 0.0),
            jnp.exp((jv + 1.0) * lg),
            jnp.exp((chunk - 1.0 - jv) * lg),
            jnp.exp(jnp.full((1, 1), float(chunk), F32) * lg),
        ))

    def rotate(t, cosf, sinf):
        return t * cosf + pltpu.roll(t, half, 1) * sinf

    def step(c, states):
        r0 = pl.multiple_of(c * chunk, chunk)
        rows = pl.ds(r0, chunk)
        cosf = cos_ref[rows, :]
        sinf = sin_ref[rows, :]
        new_states = []
        staged = []
        for hh in range(hp):
            cols = slice(hh * HEAD_DIM, (hh + 1) * HEAD_DIM)
            q = rotate(q_ref[rows, cols].astype(F32), cosf, sinf)
            k = rotate(k_ref[rows, cols].astype(F32), cosf, sinf) * scale
            scores = lax.dot_general(q.astype(BF16), k.astype(BF16), (((1,), (1,)), ((), ())),
                                     preferred_element_type=F32)
            staged.append((q, k, scores))
        for hh in range(hp):
            cols = slice(hh * HEAD_DIM, (hh + 1) * HEAD_DIM)
            intra, q_decay, k_decay, chunk_decay = decays[hh]
            state = states[hh]
            q, k, scores = staged[hh]
            v = v_ref[rows, cols]
            out = (jnp.dot((scores * intra).astype(BF16), v, preferred_element_type=F32)
                   + jnp.dot((q * q_decay).astype(BF16), state.astype(BF16), preferred_element_type=F32))
            new_states.append(state * chunk_decay + lax.dot_general(
                (k * k_decay).astype(BF16), v, (((0,), (0,)), ((), ())), preferred_element_type=F32))
            mu = jnp.mean(out, axis=1, keepdims=True)
            cen = out - mu
            var = jnp.mean(cen * cen, axis=1, keepdims=True)
            y = cen * lax.rsqrt(var + GN_EPS)
            g = g_ref[rows, cols].astype(F32)
            o_ref[rows, cols] = (y * (g / (1.0 + jnp.exp(-g)))).astype(o_ref.dtype)
        return tuple(new_states)

    zero = jnp.zeros((HEAD_DIM, HEAD_DIM), F32)
    lax.fori_loop(0, seq // chunk, step, (zero,) * hp)


def _retention(proj, log_gamma, cosf, sinf, batch, seq, chunk=256, hp=4):
    groups = RET_HEADS // hp
    width = hp * HEAD_DIM
    qoff = RET_MAIN_OFF // width
    ret_blocks = RET_WIDTH // width

    def head_spec(off):
        return pl.BlockSpec((seq, width), lambda b, g, lg: (b, off + g))

    table = pl.BlockSpec((seq, HEAD_DIM), lambda b, g, lg: (0, 0))
    return pl.pallas_call(
        functools.partial(_ret_kernel, chunk=chunk, hp=hp),
        grid_spec=pltpu.PrefetchScalarGridSpec(
            num_scalar_prefetch=1,
            grid=(batch, groups),
            in_specs=[head_spec(qoff), head_spec(qoff + ret_blocks), head_spec(qoff + 2 * ret_blocks),
                      head_spec(qoff + 3 * ret_blocks), table, table],
            out_specs=pl.BlockSpec((seq, width), lambda b, g, lg: (b, g)),
        ),
        out_shape=jax.ShapeDtypeStruct((batch * seq, RET_WIDTH), BF16),
        compiler_params=_params("parallel", "parallel"),
        name="retention",
    )(log_gamma, proj, proj, proj, proj, cosf, sinf)


def _ln_kernel(x_ref, y_ref, g_ref, gain_ref, bias_ref, *rest, alpha, modulate):
    r = alpha * x_ref[...] + g_ref[...] * y_ref[...].astype(F32)
    mu = jnp.mean(r, axis=1, keepdims=True)
    cen = r - mu
    var = jnp.mean(cen * cen, axis=1, keepdims=True)
    xn = cen * lax.rsqrt(var + LN_EPS) * gain_ref[...] + bias_ref[...]
    if modulate:
        sc_ref, sh_ref, xo_ref, h_ref = rest
        xo_ref[...] = xn
        h_ref[...] = (xn * (1.0 + sc_ref[...]) + sh_ref[...]).astype(h_ref.dtype)
    else:
        xo_ref, = rest
        xo_ref[...] = xn


def _residual_ln(x2d, branch, gate, gain, bias, sc, sh, seq, alpha, tm=256):
    m, d = x2d.shape
    per_b = seq // tm
    modulate = sc is not None
    row = pl.BlockSpec((tm, d), lambda i: (i, 0))
    vec_b = pl.BlockSpec((None, 1, d), lambda i: (i // per_b, 0, 0))
    vec = pl.BlockSpec((1, d), lambda i: (0, 0))
    in_specs = [row, row, vec_b, vec, vec]
    args = [x2d, branch, gate, gain, bias]
    out_specs = [row]
    out_shape = [jax.ShapeDtypeStruct((m, d), F32)]
    if modulate:
        in_specs += [vec_b, vec_b]
        args += [sc, sh]
        out_specs.append(row)
        out_shape.append(jax.ShapeDtypeStruct((m, d), BF16))
    res = pl.pallas_call(
        functools.partial(_ln_kernel, alpha=alpha, modulate=modulate),
        grid=(m // tm,),
        in_specs=in_specs,
        out_specs=out_specs,
        out_shape=out_shape,
        compiler_params=_params("parallel"),
        name="residual_layernorm",
    )(*args)
    return (res[0], res[1]) if modulate else (res[0], None)


def _rotary_tables(seq):
    pos = np.arange(seq, dtype=np.float32)
    inv_freq = (np.float32(ROPE_BASE) ** (-np.arange(0, HEAD_DIM, 2, dtype=np.float32) / np.float32(HEAD_DIM))
                ).astype(np.float32)
    ang = (pos[:, None] * inv_freq[None, :]).astype(np.float32)
    cos = np.cos(ang).astype(np.float32)
    sin = np.sin(ang).astype(np.float32)
    return (jnp.asarray(np.concatenate([cos, cos], axis=1)),
            jnp.asarray(np.concatenate([-sin, sin], axis=1)))


def _log_gamma():
    h = np.arange(RET_HEADS, dtype=np.float32)
    return jnp.asarray(np.log1p(-np.exp2(np.float32(-5.0) - h)).astype(np.float32))


def kernel(x, c, w_ada, b_ada, w_in, b_f, conv_w, w_out, ln1_g, ln1_b, w_up, w_down, ln2_g, ln2_b):
    batch, seq, d = x.shape
    depth = w_ada.shape[0]
    m = batch * seq
    alpha = (2.0 * depth) ** 0.25
    pad_rows = 8

    c_pad = jnp.zeros((pad_rows, d), F32).at[:batch].set(c)
    b_ada3 = b_ada.reshape(depth, 1, 6 * d)

    def split_mod(mod_rows):
        return mod_rows[:batch].reshape(batch, 6, 1, d)

    mod = split_mod(_ada_first_layer(c_pad, w_ada, b_ada3))

    cosf, sinf = _rotary_tables(seq)
    log_gamma = _log_gamma()

    w_main_t, w_f_t = _prep_in_proj_weights(w_in)
    b_f_pad = jnp.zeros((depth, 1, LANE), F32).at[:, 0, :FOX_HEADS].set(b_f)

    x2d = x.reshape(m, d)
    h = _modulate(x2d, mod[:, 1], mod[:, 0], seq)
    for l in range(depth):
        sh_m, sc_m, g_a, g_m = mod[:, 3], mod[:, 4], mod[:, 2], mod[:, 5]
        last = l + 1 == depth

        proj, w_up_l = _in_proj(h, w_main_t, w_up, l)
        f_logit = _matmul_nt(h, w_f_t, l, tm=1024, tn=LANE, out_dtype=F32, name="forget_logits")

        cum = _forget_cumsum(f_logit, b_f_pad[l], batch, seq, rows=FOX_TK)
        y_conv = _short_conv(proj, conv_w[l], batch, seq)
        y_fox = _forgetting_attention(proj, cum, batch, seq, tq=FOX_TQ, tk=FOX_TK, hp=FOX_HP)
        y_ret = _retention(proj, log_gamma, cosf, sinf, batch, seq)

        mix = _out_proj(y_conv, y_fox, y_ret, w_out, l)
        x2d, h2 = _residual_ln(x2d, mix, g_a, ln1_g[l][None], ln1_b[l][None], sc_m, sh_m, seq, alpha)

        a, w_down_l, mod_next = _mlp_up(h2, w_up_l, w_down, l,
                                        ada_next=None if last else (c_pad, w_ada, b_ada3))
        mlp = _matmul_ksplit(a, w_down_l, tm=1024, tn=1024, tk=d, out_dtype=BF16, name="mlp_down")
        if last:
            x2d, _ = _residual_ln(x2d, mlp, g_m, ln2_g[l][None], ln2_b[l][None], None, None, seq, alpha)
        else:
            mod = split_mod(mod_next)
            x2d, h = _residual_ln(x2d, mlp, g_m, ln2_g[l][None], ln2_b[l][None],
                                  mod[:, 1], mod[:, 0], seq, alpha)
    return x2d.reshape(batch, seq, d)
```

```python
import functools

import numpy as np
import jax
import jax.numpy as jnp
from jax import lax
from jax.experimental import pallas as pl
from jax.experimental.pallas import tpu as pltpu

D_MODEL = 4096
HEAD_DIM = 128
CONV_WIDTH = D_MODEL // 4
FOX_WIDTH = (D_MODEL - CONV_WIDTH) // 2
RET_WIDTH = D_MODEL - CONV_WIDTH - FOX_WIDTH
FOX_HEADS = FOX_WIDTH // HEAD_DIM
RET_HEADS = RET_WIDTH // HEAD_DIM
CONV_K = 3
ROPE_BASE = 10000.0
LN_EPS = 1e-5
GN_EPS = 1e-6

FOX_OFF = 3 * CONV_WIDTH
FLOGIT_OFF = FOX_OFF + 3 * FOX_WIDTH
RET_OFF = FLOGIT_OFF + FOX_HEADS
MAIN_WIDTH = 3 * CONV_WIDTH + 3 * FOX_WIDTH + 4 * RET_WIDTH
RET_MAIN_OFF = FOX_OFF + 3 * FOX_WIDTH

LANE = 128
LOG2_E = 1.4426950408889634

FOX_TQ = 1024
FOX_TK = 512
FOX_HP = 2
FOX_QSPLIT = 2
ONES_ROWS = 16
VMEM_LIMIT = 56 * 1024 * 1024

BF16 = jnp.bfloat16
F32 = jnp.float32


def _params(*sem):
    return pltpu.CompilerParams(dimension_semantics=sem, vmem_limit_bytes=VMEM_LIMIT)


def _ada_block(c_ref, w_ref, b_ref):
    c = c_ref[...]
    cond = (c / (1.0 + jnp.exp(-c))).astype(BF16)
    return jnp.dot(cond, w_ref[...].astype(BF16), preferred_element_type=F32) + b_ref[...]


def _ada_kernel(c_ref, w_ref, b_ref, o_ref):
    o_ref[...] = _ada_block(c_ref, w_ref, b_ref)


def _ada_first_layer(c_pad, w_ada, b_ada3, tn=512):
    _, d, n = w_ada.shape
    rows = c_pad.shape[0]
    return pl.pallas_call(
        _ada_kernel,
        grid=(n // tn,),
        in_specs=[
            pl.BlockSpec((rows, d), lambda j: (0, 0)),
            pl.BlockSpec((None, d, tn), lambda j: (0, 0, j)),
            pl.BlockSpec((None, 1, tn), lambda j: (0, 0, j)),
        ],
        out_specs=pl.BlockSpec((rows, tn), lambda j: (0, j)),
        out_shape=jax.ShapeDtypeStruct((rows, n), F32),
        compiler_params=_params("parallel"),
        name="ada_modulation",
    )(c_pad, w_ada, b_ada3)


def _modulate_kernel(x_ref, sc_ref, sh_ref, h_ref):
    h_ref[...] = (x_ref[...] * (1.0 + sc_ref[...]) + sh_ref[...]).astype(h_ref.dtype)


def _modulate(x2d, sc, sh, seq, tm=512):
    m, d = x2d.shape
    per_b = seq // tm
    vec = pl.BlockSpec((None, 1, d), lambda i: (i // per_b, 0, 0))
    return pl.pallas_call(
        _modulate_kernel,
        grid=(m // tm,),
        in_specs=[pl.BlockSpec((tm, d), lambda i: (i, 0)), vec, vec],
        out_specs=pl.BlockSpec((tm, d), lambda i: (i, 0)),
        out_shape=jax.ShapeDtypeStruct((m, d), BF16),
        compiler_params=_params("parallel"),
        name="modulate",
    )(x2d, sc, sh)


def _prep_main_kernel(a_ref, b_ref, o_ref, *, first_shifted, shift):
    @pl.when(pl.program_id(0) < first_shifted)
    def _():
        o_ref[...] = jnp.swapaxes(a_ref[...], 0, 1).astype(o_ref.dtype)

    @pl.when(pl.program_id(0) >= first_shifted)
    def _():
        rows = jnp.concatenate([a_ref[shift:, :, :], b_ref[:shift, :, :]], axis=0)
        o_ref[...] = jnp.swapaxes(rows, 0, 1).astype(o_ref.dtype)


def _prep_flogit_kernel(a_ref, o_ref, *, count):
    tail = a_ref.shape[0]
    v = jnp.swapaxes(a_ref[...], 0, 1)
    keep = lax.broadcasted_iota(jnp.int32, v.shape, 1) < count
    o_ref[...] = jnp.zeros_like(o_ref)
    o_ref[:, :tail, :] = jnp.where(keep, v, 0.0).astype(o_ref.dtype)


def _prep_in_proj_weights(w_in, rows=256, tail=16):
    depth, d, _ = w_in.shape
    w_nlk = jnp.transpose(w_in, (2, 0, 1))
    shift = RET_OFF - FLOGIT_OFF
    w_main_t = pl.pallas_call(
        functools.partial(_prep_main_kernel, first_shifted=FLOGIT_OFF // rows, shift=shift),
        grid=(MAIN_WIDTH // rows,),
        in_specs=[
            pl.BlockSpec((rows, depth, d), lambda i: (i, 0, 0)),
            pl.BlockSpec((tail, depth, d), lambda i: ((i + 1) * (rows // tail), 0, 0)),
        ],
        out_specs=pl.BlockSpec((depth, rows, d), lambda i: (0, i, 0)),
        out_shape=jax.ShapeDtypeStruct((depth, MAIN_WIDTH, d), BF16),
        compiler_params=_params("parallel"),
        name="prep_in_proj_weights",
    )(w_nlk, w_nlk)
    w_f_t = pl.pallas_call(
        functools.partial(_prep_flogit_kernel, count=FOX_HEADS),
        grid=(1,),
        in_specs=[pl.BlockSpec((tail, depth, d), lambda i: (FLOGIT_OFF // tail, 0, 0))],
        out_specs=pl.BlockSpec((depth, LANE, d), lambda i: (0, 0, 0)),
        out_shape=jax.ShapeDtypeStruct((depth, LANE, d), BF16),
        compiler_params=_params("arbitrary"),
        name="prep_forget_logit_weights",
    )(w_nlk)
    return w_main_t, w_f_t


def _mm_nt_kernel(a_ref, wt_ref, o_ref):
    acc = lax.dot_general(a_ref[...], wt_ref[...], (((1,), (1,)), ((), ())),
                          preferred_element_type=F32)
    o_ref[...] = acc.astype(o_ref.dtype)


def _matmul_nt(a, wt, layer, *, tm, tn, out_dtype, name):
    m, kdim = a.shape
    n = wt.shape[1]
    return pl.pallas_call(
        _mm_nt_kernel,
        grid=(m // tm, n // tn),
        in_specs=[
            pl.BlockSpec((tm, kdim), lambda i, j: (i, 0)),
            pl.BlockSpec((None, tn, kdim), lambda i, j: (layer, j, 0)),
        ],
        out_specs=pl.BlockSpec((tm, tn), lambda i, j: (i, j)),
        out_shape=jax.ShapeDtypeStruct((m, n), out_dtype),
        compiler_params=_params("parallel", "parallel"),
        name=name,
    )(a, wt)


def _mm_ksplit_kernel(a_ref, w_ref, o_ref, acc_ref, *, nk):
    k = pl.program_id(2)

    @pl.when(k == 0)
    def _():
        acc_ref[...] = jnp.dot(a_ref[...], w_ref[...], preferred_element_type=F32)

    @pl.when(k > 0)
    def _():
        acc_ref[...] += jnp.dot(a_ref[...], w_ref[...], preferred_element_type=F32)

    @pl.when(k == nk - 1)
    def _():
        o_ref[...] = acc_ref[...].astype(o_ref.dtype)


def _matmul_ksplit(a, w, *, tm, tn, tk, out_dtype, name):
    m, kdim = a.shape
    n = w.shape[1]
    nk = kdim // tk
    assert nk >= 2
    return pl.pallas_call(
        functools.partial(_mm_ksplit_kernel, nk=nk),
        grid=(m // tm, n // tn, nk),
        in_specs=[
            pl.BlockSpec((tm, tk), lambda i, j, k: (i, k)),
            pl.BlockSpec((tk, tn), lambda i, j, k: (k, j)),
        ],
        out_specs=pl.BlockSpec((tm, tn), lambda i, j, k: (i, j)),
        out_shape=jax.ShapeDtypeStruct((m, n), out_dtype),
        scratch_shapes=[pltpu.VMEM((tm, tn), F32)],
        compiler_params=_params("parallel", "parallel", "arbitrary"),
        name=name,
    )(a, w)


def _mlp_up_kernel(h_ref, w_ref, wd_ref, *rest, ada_steps):
    if ada_steps:
        c_ref, wa_ref, ba_ref, a_ref, wd_bf16_ref, mod_ref = rest
    else:
        a_ref, wd_bf16_ref = rest
    acc = jnp.dot(h_ref[...], w_ref[...].astype(BF16), preferred_element_type=F32)
    a_ref[...] = jnp.square(jnp.maximum(acc, 0.0)).astype(a_ref.dtype)
    wd_bf16_ref[...] = wd_ref[...].astype(wd_bf16_ref.dtype)
    if ada_steps:
        step = pl.program_id(0) * pl.num_programs(1) + pl.program_id(1)

        @pl.when(step < ada_steps)
        def _():
            mod_ref[...] = _ada_block(c_ref, wa_ref, ba_ref)


def _mlp_up(h, w_up, w_down, layer, ada_next=None, tm=2048, tn=512, ada_tn=256):
    m, d = h.shape
    d_ff = w_up.shape[-1]
    n_i, n_j = m // tm, d_ff // tn
    slab = d_ff // (n_i * n_j)
    in_specs = [
        pl.BlockSpec((tm, d), lambda i, j: (i, 0), pipeline_mode=pl.Buffered(1)),
        pl.BlockSpec((None, d, tn), lambda i, j: (layer, 0, j)),
        pl.BlockSpec((None, slab, d), lambda i, j: (layer, i * n_j + j, 0)),
    ]
    out_specs = [
        pl.BlockSpec((tm, tn), lambda i, j: (i, j)),
        pl.BlockSpec((slab, d), lambda i, j: (i * n_j + j, 0)),
    ]
    out_shape = [jax.ShapeDtypeStruct((m, d_ff), BF16), jax.ShapeDtypeStruct((d_ff, d), BF16)]
    args = [h, w_up, w_down]
    ada_steps = 0
    if ada_next is not None:
        c_pad, w_ada, b_ada3 = ada_next
        rows, n_mod = c_pad.shape[0], w_ada.shape[-1]
        ada_steps = n_mod // ada_tn
        assert ada_steps <= n_i * n_j

        def ada_block(i, j):
            return jnp.minimum(i * n_j + j, ada_steps - 1)

        in_specs += [
            pl.BlockSpec((rows, d), lambda i, j: (0, 0)),
            pl.BlockSpec((None, d, ada_tn), lambda i, j: (layer + 1, 0, ada_block(i, j))),
            pl.BlockSpec((None, 1, ada_tn), lambda i, j: (layer + 1, 0, ada_block(i, j))),
        ]
        out_specs.append(pl.BlockSpec((rows, ada_tn), lambda i, j: (0, ada_block(i, j))))
        out_shape.append(jax.ShapeDtypeStruct((rows, n_mod), F32))
        args += [c_pad, w_ada, b_ada3]
    res = pl.pallas_call(
        functools.partial(_mlp_up_kernel, ada_steps=ada_steps),
        grid=(n_i, n_j),
        in_specs=in_specs,
        out_specs=out_specs,
        out_shape=out_shape,
        compiler_params=_params("arbitrary", "arbitrary") if ada_steps else _params("parallel", "parallel"),
        name="mlp_up",
    )(*args)
    return (res[0], res[1], res[2]) if ada_steps else (res[0], res[1], None)


def _out_proj_kernel(yc_ref, yf_ref, yr_ref, w_ref, o_ref):
    f0 = yc_ref.shape[1]
    r0 = f0 + yf_ref.shape[1]
    acc = jnp.dot(yc_ref[...], w_ref[0:f0, :].astype(BF16), preferred_element_type=F32)
    acc += jnp.dot(yf_ref[...], w_ref[f0:r0, :].astype(BF16), preferred_element_type=F32)
    acc += jnp.dot(yr_ref[...], w_ref[r0:, :].astype(BF16), preferred_element_type=F32)
    o_ref[...] = acc.astype(o_ref.dtype)


def _out_proj(y_conv, y_fox, y_ret, w, layer, tm=1024, tn=512):
    m = y_conv.shape[0]
    _, kdim, n = w.shape

    def lhs(width):
        return pl.BlockSpec((tm, width), lambda i, j: (i, 0))

    return pl.pallas_call(
        _out_proj_kernel,
        grid=(m // tm, n // tn),
        in_specs=[lhs(y_conv.shape[1]), lhs(y_fox.shape[1]), lhs(y_ret.shape[1]),
                  pl.BlockSpec((None, kdim, tn), lambda i, j: (layer, 0, j))],
        out_specs=pl.BlockSpec((tm, tn), lambda i, j: (i, j)),
        out_shape=jax.ShapeDtypeStruct((m, n), BF16),
        compiler_params=_params("parallel", "parallel"),
        name="out_proj",
    )(y_conv, y_fox, y_ret, w)


def _conv_kernel(u_ref, b_ref, c_ref, w_ref, o_ref, *, rows):
    seq, tc = u_ref.shape
    w0 = w_ref[0:1, :]
    w1 = w_ref[1:2, :]
    w2 = w_ref[2:3, :]
    rid = lax.broadcasted_iota(jnp.int32, (rows, 1), 0)

    def chunk(ci, carry):
        prev2, prev1 = carry
        r0 = pl.multiple_of(ci * rows, rows)
        z = c_ref[pl.ds(r0, rows), :].astype(F32) * u_ref[pl.ds(r0, rows), :].astype(F32)
        z1 = jnp.where(rid == 0, prev1, pltpu.roll(z, 1, 0))
        z2 = jnp.where(rid == 0, prev2, jnp.where(rid == 1, prev1, pltpu.roll(z, 2, 0)))
        y = w0 * z2 + w1 * z1 + w2 * z
        o_ref[pl.ds(r0, rows), :] = (b_ref[pl.ds(r0, rows), :].astype(F32) * y).astype(o_ref.dtype)
        return z[rows - 2:rows - 1, :], z[rows - 1:rows, :]

    zero = jnp.zeros((1, tc), F32)
    lax.fori_loop(0, seq // rows, chunk, (zero, zero))


def _short_conv(proj, conv_w, batch, seq, tc=256, rows=512):
    nblk = CONV_WIDTH // tc
    return pl.pallas_call(
        functools.partial(_conv_kernel, rows=rows),
        grid=(batch, nblk),
        in_specs=[
            pl.BlockSpec((seq, tc), lambda b, j: (b, j)),
            pl.BlockSpec((seq, tc), lambda b, j: (b, nblk + j)),
            pl.BlockSpec((seq, tc), lambda b, j: (b, 2 * nblk + j)),
            pl.BlockSpec((CONV_K, tc), lambda b, j: (0, j)),
        ],
        out_specs=pl.BlockSpec((seq, tc), lambda b, j: (b, j)),
        out_shape=jax.ShapeDtypeStruct((batch * seq, CONV_WIDTH), BF16),
        compiler_params=_params("parallel", "parallel"),
        name="short_conv",
    )(proj, proj, proj, conv_w)


def _cum_kernel(f_ref, bf_ref, o_ref, carry_ref):
    rows = f_ref.shape[0]

    @pl.when(pl.program_id(1) == 0)
    def _():
        carry_ref[...] = jnp.zeros_like(carry_ref)

    z = f_ref[...] + bf_ref[...]
    logf = jnp.minimum(z, 0.0) - jnp.log1p(jnp.exp(-jnp.abs(z)))
    p1 = logf.astype(BF16)
    r1 = logf - p1.astype(F32)
    p2 = r1.astype(BF16)
    p3 = (r1 - p2.astype(F32)).astype(BF16)
    ri = lax.broadcasted_iota(jnp.int32, (rows, rows), 0)
    ci = lax.broadcasted_iota(jnp.int32, (rows, rows), 1)
    tri = (ri >= ci).astype(BF16)
    cum = (jnp.dot(tri, p1, preferred_element_type=F32)
           + jnp.dot(tri, p2, preferred_element_type=F32)
           + jnp.dot(tri, p3, preferred_element_type=F32)) + carry_ref[...]
    o_ref[...] = cum.T
    carry_ref[...] = cum[rows - 1:rows, :]


def _forget_cumsum(f_logit, b_f_pad, batch, seq, rows):
    per_b = seq // rows
    return pl.pallas_call(
        _cum_kernel,
        grid=(batch, per_b),
        in_specs=[
            pl.BlockSpec((rows, LANE), lambda b, j: (b * per_b + j, 0)),
            pl.BlockSpec((1, LANE), lambda b, j: (0, 0)),
        ],
        out_specs=pl.BlockSpec((None, None, LANE, rows), lambda b, j: (b, j, 0, 0)),
        out_shape=jax.ShapeDtypeStruct((batch, per_b, LANE, rows), F32),
        scratch_shapes=[pltpu.VMEM((1, LANE), F32)],
        compiler_params=_params("parallel", "arbitrary"),
        name="forget_cumsum",
    )(f_logit, b_f_pad)


def _fox_kernel(q_ref, k_ref, v_ref, f_ref, o_ref, vt_ref, frep_ref, m_ref, acc_ref, *, tq, tk, hp):
    seq = q_ref.shape[0]
    inv_scale = HEAD_DIM ** 0.5
    exp2_scale = HEAD_DIM ** -0.5 * LOG2_E
    head0 = pl.program_id(1) * hp
    per_q = tq // tk
    rel_pos = (lax.broadcasted_iota(jnp.int32, (tk, tq), 1)
               - lax.broadcasted_iota(jnp.int32, (tk, tq), 0))

    def head_cols(hh):
        return slice(hh * HEAD_DIM, (hh + 1) * HEAD_DIM)

    for hh in range(hp):
        def prepare(j, carry, hh=hh):
            r0 = pl.multiple_of(j * tk, tk)
            vt_ref[hh, j, :HEAD_DIM] = v_ref[pl.ds(r0, tk), head_cols(hh)].astype(F32).T.astype(BF16)
            vt_ref[hh, j, HEAD_DIM:] = (
                lax.broadcasted_iota(jnp.int32, (ONES_ROWS, tk), 0) == 0).astype(BF16)
            f_row = f_ref[j, pl.ds(head0 + hh, 1), :] * inv_scale
            frep_ref[hh, j] = jnp.broadcast_to(f_row, (LANE, tk)).T
            return carry

        lax.fori_loop(0, seq // tk, prepare, 0)

    def q_block(qi, _):
        q0 = pl.multiple_of(qi * tq, tq)
        m_ref[...] = jnp.full_like(m_ref, -jnp.inf)
        acc_ref[...] = jnp.zeros_like(acc_ref)

        def kv_block(ki, diag):
            k0 = pl.multiple_of(ki * tk, tk)
            sub = tq // FOX_QSPLIT
            units = [(hh, c) for hh in range(hp) for c in range(FOX_QSPLIT)]
            scores = []
            for hh, c in units:
                q = q_ref[pl.ds(q0 + c * sub, sub), head_cols(hh)]
                k = k_ref[pl.ds(k0, tk), head_cols(hh)]
                scores.append(lax.dot_general(k, q, (((1,), (1,)), ((), ())),
                                              preferred_element_type=F32))
            for (hh, c), st in zip(units, scores):
                cols = slice(c * sub, (c + 1) * sub)
                u = st - jnp.concatenate([frep_ref[hh, ki]] * (sub // LANE), axis=1)
                if diag is not None:
                    u = jnp.where(rel_pos[:, cols] >= diag * tk, u, -jnp.inf)
                m_old = m_ref[hh, :, cols]
                m_new = jnp.maximum(m_old, jnp.max(u, axis=0, keepdims=True))
                p = jnp.exp2((u - m_new) * exp2_scale)
                alpha = jnp.exp2((m_old - m_new) * exp2_scale)
                acc_ref[hh, :, cols] = alpha * acc_ref[hh, :, cols] + jnp.dot(
                    vt_ref[hh, ki], p.astype(BF16), preferred_element_type=F32)
                m_ref[hh, :, cols] = m_new

        def full_block(ki, carry):
            kv_block(ki, None)
            return carry

        lax.fori_loop(0, qi * per_q, full_block, 0)
        for d in range(per_q):
            kv_block(qi * per_q + d, d)
        for hh in range(hp):
            out_t = acc_ref[hh, :HEAD_DIM] / acc_ref[hh, HEAD_DIM:HEAD_DIM + 1]
            o_ref[pl.ds(q0, tq), head_cols(hh)] = out_t.T.astype(o_ref.dtype)
        return 0

    lax.fori_loop(0, seq // tq, q_block, 0)


def _forgetting_attention(proj, cum, batch, seq, tq, tk, hp):
    groups = FOX_HEADS // hp
    width = hp * HEAD_DIM
    qoff = FOX_OFF // width
    fox_blocks = FOX_WIDTH // width
    head_rows = 16

    def head_spec(off):
        return pl.BlockSpec((seq, width), lambda b, g: (b, off + g))

    return pl.pallas_call(
        functools.partial(_fox_kernel, tq=tq, tk=tk, hp=hp),
        grid=(batch, groups),
        in_specs=[
            head_spec(qoff), head_spec(qoff + fox_blocks), head_spec(qoff + 2 * fox_blocks),
            pl.BlockSpec((None, seq // tk, head_rows, tk), lambda b, g: (b, 0, 0, 0)),
        ],
        out_specs=pl.BlockSpec((seq, width), lambda b, g: (b, g)),
        out_shape=jax.ShapeDtypeStruct((batch * seq, FOX_WIDTH), BF16),
        scratch_shapes=[
            pltpu.VMEM((hp, seq // tk, HEAD_DIM + ONES_ROWS, tk), BF16),
            pltpu.VMEM((hp, seq // tk, tk, LANE), F32),
            pltpu.VMEM((hp, 1, tq), F32),
            pltpu.VMEM((hp, HEAD_DIM + ONES_ROWS, tq), F32),
        ],
        compiler_params=_params("parallel", "parallel"),
        name="forgetting_attention",
    )(proj, proj, proj, cum)


def _ret_kernel(lg_ref, q_ref, k_ref, v_ref, g_ref, cos_ref, sin_ref, o_ref, *, chunk, hp):
    seq = q_ref.shape[0]
    scale = HEAD_DIM ** -0.5
    half = HEAD_DIM // 2
    ri = lax.broadcasted_iota(jnp.int32, (chunk, chunk), 0)
    ci = lax.broadcasted_iota(jnp.int32, (chunk, chunk), 1)
    diff = (ri - ci).astype(F32)
    jv = lax.broadcasted_iota(jnp.int32, (chunk, 1), 0).astype(F32)
    decays = []
    for hh in range(hp):
        lg = lg_ref[pl.program_id(1) * hp + hh]
        decays.append((
            jnp.where(diff >= 0.0, jnp.exp(jnp.maximum(diff, 0.0) * lg), 0.0),
            jnp.exp((jv + 1.0) * lg),
            jnp.exp((chunk - 1.0 - jv) * lg),
            jnp.exp(jnp.full((1, 1), float(chunk), F32) * lg),
        ))

    def rotate(t, cosf, sinf):
        return t * cosf + pltpu.roll(t, half, 1) * sinf

    def step(c, states):
        r0 = pl.multiple_of(c * chunk, chunk)
        rows = pl.ds(r0, chunk)
        cosf = cos_ref[rows, :]
        sinf = sin_ref[rows, :]
        new_states = []
        staged = []
        for hh in range(hp):
            cols = slice(hh * HEAD_DIM, (hh + 1) * HEAD_DIM)
            q = rotate(q_ref[rows, cols].astype(F32), cosf, sinf)
            k = rotate(k_ref[rows, cols].astype(F32), cosf, sinf) * scale
            scores = lax.dot_general(q.astype(BF16), k.astype(BF16), (((1,), (1,)), ((), ())),
                                     preferred_element_type=F32)
            staged.append((q, k, scores))
        for hh in range(hp):
            cols = slice(hh * HEAD_DIM, (hh + 1) * HEAD_DIM)
            intra, q_decay, k_decay, chunk_decay = decays[hh]
            state = states[hh]
            q, k, scores = staged[hh]
            v = v_ref[rows, cols]
            out = (jnp.dot((scores * intra).astype(BF16), v, preferred_element_type=F32)
                   + jnp.dot((q * q_decay).astype(BF16), state.astype(BF16), preferred_element_type=F32))
            new_states.append(state * chunk_decay + lax.dot_general(
                (k * k_decay).astype(BF16), v, (((0,), (0,)), ((), ())), preferred_element_type=F32))
            mu = jnp.mean(out, axis=1, keepdims=True)
            cen = out - mu
            var = jnp.mean(cen * cen, axis=1, keepdims=True)
            y = cen * lax.rsqrt(var + GN_EPS)
            g = g_ref[rows, cols].astype(F32)
            o_ref[rows, cols] = (y * (g / (1.0 + jnp.exp(-g)))).astype(o_ref.dtype)
        return tuple(new_states)

    zero = jnp.zeros((HEAD_DIM, HEAD_DIM), F32)
    lax.fori_loop(0, seq // chunk, step, (zero,) * hp)


def _retention(proj, log_gamma, cosf, sinf, batch, seq, chunk=256, hp=4):
    groups = RET_HEADS // hp
    width = hp * HEAD_DIM
    qoff = RET_MAIN_OFF // width
    ret_blocks = RET_WIDTH // width

    def head_spec(off):
        return pl.BlockSpec((seq, width), lambda b, g, lg: (b, off + g))

    table = pl.BlockSpec((seq, HEAD_DIM), lambda b, g, lg: (0, 0))
    return pl.pallas_call(
        functools.partial(_ret_kernel, chunk=chunk, hp=hp),
        grid_spec=pltpu.PrefetchScalarGridSpec(
            num_scalar_prefetch=1,
            grid=(batch, groups),
            in_specs=[head_spec(qoff), head_spec(qoff + ret_blocks), head_spec(qoff + 2 * ret_blocks),
                      head_spec(qoff + 3 * ret_blocks), table, table],
            out_specs=pl.BlockSpec((seq, width), lambda b, g, lg: (b, g)),
        ),
        out_shape=jax.ShapeDtypeStruct((batch * seq, RET_WIDTH), BF16),
        compiler_params=_params("parallel", "parallel"),
        name="retention",
    )(log_gamma, proj, proj, proj, proj, cosf, sinf)


def _ln_kernel(x_ref, y_ref, g_ref, gain_ref, bias_ref, *rest, alpha, modulate):
    r = alpha * x_ref[...] + g_ref[...] * y_ref[...].astype(F32)
    mu = jnp.mean(r, axis=1, keepdims=True)
    cen = r - mu
    var = jnp.mean(cen * cen, axis=1, keepdims=True)
    xn = cen * lax.rsqrt(var + LN_EPS) * gain_ref[...] + bias_ref[...]
    if modulate:
        sc_ref, sh_ref, xo_ref, h_ref = rest
        xo_ref[...] = xn
        h_ref[...] = (xn * (1.0 + sc_ref[...]) + sh_ref[...]).astype(h_ref.dtype)
    else:
        xo_ref, = rest
        xo_ref[...] = xn


def _residual_ln(x2d, branch, gate, gain, bias, sc, sh, seq, alpha, tm=256):
    m, d = x2d.shape
    per_b = seq // tm
    modulate = sc is not None
    row = pl.BlockSpec((tm, d), lambda i: (i, 0))
    vec_b = pl.BlockSpec((None, 1, d), lambda i: (i // per_b, 0, 0))
    vec = pl.BlockSpec((1, d), lambda i: (0, 0))
    in_specs = [row, row, vec_b, vec, vec]
    args = [x2d, branch, gate, gain, bias]
    out_specs = [row]
    out_shape = [jax.ShapeDtypeStruct((m, d), F32)]
    if modulate:
        in_specs += [vec_b, vec_b]
        args += [sc, sh]
        out_specs.append(row)
        out_shape.append(jax.ShapeDtypeStruct((m, d), BF16))
    res = pl.pallas_call(
        functools.partial(_ln_kernel, alpha=alpha, modulate=modulate),
        grid=(m // tm,),
        in_specs=in_specs,
        out_specs=out_specs,
        out_shape=out_shape,
        compiler_params=_params("parallel"),
        name="residual_layernorm",
    )(*args)
    return (res[0], res[1]) if modulate else (res[0], None)


def _rotary_tables(seq):
    pos = np.arange(seq, dtype=np.float32)
    inv_freq = (np.float32(ROPE_BASE) ** (-np.arange(0, HEAD_DIM, 2, dtype=np.float32) / np.float32(HEAD_DIM))
                ).astype(np.float32)
    ang = (pos[:, None] * inv_freq[None, :]).astype(np.float32)
    cos = np.cos(ang).astype(np.float32)
    sin = np.sin(ang).astype(np.float32)
    return (jnp.asarray(np.concatenate([cos, cos], axis=1)),
            jnp.asarray(np.concatenate([-sin, sin], axis=1)))


def _log_gamma():
    h = np.arange(RET_HEADS, dtype=np.float32)
    return jnp.asarray(np.log1p(-np.exp2(np.float32(-5.0) - h)).astype(np.float32))


def kernel(x, c, w_ada, b_ada, w_in, b_f, conv_w, w_out, ln1_g, ln1_b, w_up, w_down, ln2_g, ln2_b):
    batch, seq, d = x.shape
    depth = w_ada.shape[0]
    m = batch * seq
    alpha = (2.0 * depth) ** 0.25
    pad_rows = 8

    c_pad = jnp.zeros((pad_rows, d), F32).at[:batch].set(c)
    b_ada3 = b_ada.reshape(depth, 1, 6 * d)

    def split_mod(mod_rows):
        return mod_rows[:batch].reshape(batch, 6, 1, d)

    mod = split_mod(_ada_first_layer(c_pad, w_ada, b_ada3))

    cosf, sinf = _rotary_tables(seq)
    log_gamma = _log_gamma()

    w_main_t, w_f_t = _prep_in_proj_weights(w_in)
    b_f_pad = jnp.zeros((depth, 1, LANE), F32).at[:, 0, :FOX_HEADS].set(b_f)

    x2d = x.reshape(m, d)
    h = _modulate(x2d, mod[:, 1], mod[:, 0], seq)
    for l in range(depth):
        sh_m, sc_m, g_a, g_m = mod[:, 3], mod[:, 4], mod[:, 2], mod[:, 5]
        last = l + 1 == depth

        proj = _matmul_nt(h, w_main_t, l, tm=1024, tn=1536, out_dtype=BF16, name="in_proj")
        f_logit = _matmul_nt(h, w_f_t, l, tm=1024, tn=LANE, out_dtype=F32, name="forget_logits")

        cum = _forget_cumsum(f_logit, b_f_pad[l], batch, seq, rows=FOX_TK)
        y_conv = _short_conv(proj, conv_w[l], batch, seq)
        y_fox = _forgetting_attention(proj, cum, batch, seq, tq=FOX_TQ, tk=FOX_TK, hp=FOX_HP)
        y_ret = _retention(proj, log_gamma, cosf, sinf, batch, seq)

        mix = _out_proj(y_conv, y_fox, y_ret, w_out, l)
        x2d, h2 = _residual_ln(x2d, mix, g_a, ln1_g[l][None], ln1_b[l][None], sc_m, sh_m, seq, alpha)

        a, w_down_l, mod_next = _mlp_up(h2, w_up, w_down, l,
                                        ada_next=None if last else (c_pad, w_ada, b_ada3))
        mlp = _matmul_ksplit(a, w_down_l, tm=1024, tn=1024, tk=d, out_dtype=BF16, name="mlp_down")
        if last:
            x2d, _ = _residual_ln(x2d, mlp, g_m, ln2_g[l][None], ln2_b[l][None], None, None, seq, alpha)
        else:
            mod = split_mod(mod_next)
            x2d, h = _residual_ln(x2d, mlp, g_m, ln2_g[l][None], ln2_b[l][None],
                                  mod[:, 1], mod[:, 0], seq, alpha)
    return x2d.reshape(batch, seq, d)
```

```python
import functools

import numpy as np
import jax
import jax.numpy as jnp
from jax import lax
from jax.experimental import pallas as pl
from jax.experimental.pallas import tpu as pltpu

D_MODEL = 4096
HEAD_DIM = 128
CONV_WIDTH = D_MODEL // 4
FOX_WIDTH = (D_MODEL - CONV_WIDTH) // 2
RET_WIDTH = D_MODEL - CONV_WIDTH - FOX_WIDTH
FOX_HEADS = FOX_WIDTH // HEAD_DIM
RET_HEADS = RET_WIDTH // HEAD_DIM
CONV_K = 3
ROPE_BASE = 10000.0
LN_EPS = 1e-5
GN_EPS = 1e-6

FOX_OFF = 3 * CONV_WIDTH
FLOGIT_OFF = FOX_OFF + 3 * FOX_WIDTH
RET_OFF = FLOGIT_OFF + FOX_HEADS
MAIN_WIDTH = 3 * CONV_WIDTH + 3 * FOX_WIDTH + 4 * RET_WIDTH
RET_MAIN_OFF = FOX_OFF + 3 * FOX_WIDTH

LANE = 128
LOG2_E = 1.4426950408889634

FOX_TQ = 1024
FOX_TK = 512
FOX_HP = 2
FOX_QSPLIT = 2
ONES_ROWS = 16
VMEM_LIMIT = 56 * 1024 * 1024

BF16 = jnp.bfloat16
F32 = jnp.float32


def _params(*sem):
    return pltpu.CompilerParams(dimension_semantics=sem, vmem_limit_bytes=VMEM_LIMIT)


def _ada_block(c_ref, w_ref, b_ref):
    c = c_ref[...]
    cond = (c / (1.0 + jnp.exp(-c))).astype(BF16)
    return jnp.dot(cond, w_ref[...].astype(BF16), preferred_element_type=F32) + b_ref[...]


def _ada_kernel(c_ref, w_ref, b_ref, o_ref):
    o_ref[...] = _ada_block(c_ref, w_ref, b_ref)


def _ada_first_layer(c_pad, w_ada, b_ada3, tn=512):
    _, d, n = w_ada.shape
    rows = c_pad.shape[0]
    return pl.pallas_call(
        _ada_kernel,
        grid=(n // tn,),
        in_specs=[
            pl.BlockSpec((rows, d), lambda j: (0, 0)),
            pl.BlockSpec((None, d, tn), lambda j: (0, 0, j)),
            pl.BlockSpec((None, 1, tn), lambda j: (0, 0, j)),
        ],
        out_specs=pl.BlockSpec((rows, tn), lambda j: (0, j)),
        out_shape=jax.ShapeDtypeStruct((rows, n), F32),
        compiler_params=_params("parallel"),
        name="ada_modulation",
    )(c_pad, w_ada, b_ada3)


def _modulate_kernel(x_ref, sc_ref, sh_ref, h_ref):
    h_ref[...] = (x_ref[...] * (1.0 + sc_ref[...]) + sh_ref[...]).astype(h_ref.dtype)


def _modulate(x2d, sc, sh, seq, tm=512):
    m, d = x2d.shape
    per_b = seq // tm
    vec = pl.BlockSpec((None, 1, d), lambda i: (i // per_b, 0, 0))
    return pl.pallas_call(
        _modulate_kernel,
        grid=(m // tm,),
        in_specs=[pl.BlockSpec((tm, d), lambda i: (i, 0)), vec, vec],
        out_specs=pl.BlockSpec((tm, d), lambda i: (i, 0)),
        out_shape=jax.ShapeDtypeStruct((m, d), BF16),
        compiler_params=_params("parallel"),
        name="modulate",
    )(x2d, sc, sh)


def _prep_main_kernel(a_ref, b_ref, o_ref, *, first_shifted, shift):
    @pl.when(pl.program_id(0) < first_shifted)
    def _():
        o_ref[...] = jnp.swapaxes(a_ref[...], 0, 1).astype(o_ref.dtype)

    @pl.when(pl.program_id(0) >= first_shifted)
    def _():
        rows = jnp.concatenate([a_ref[shift:, :, :], b_ref[:shift, :, :]], axis=0)
        o_ref[...] = jnp.swapaxes(rows, 0, 1).astype(o_ref.dtype)


def _prep_flogit_kernel(a_ref, o_ref, *, count):
    tail = a_ref.shape[0]
    v = jnp.swapaxes(a_ref[...], 0, 1)
    keep = lax.broadcasted_iota(jnp.int32, v.shape, 1) < count
    o_ref[...] = jnp.zeros_like(o_ref)
    o_ref[:, :tail, :] = jnp.where(keep, v, 0.0).astype(o_ref.dtype)


def _prep_in_proj_weights(w_in, rows=256, tail=16):
    depth, d, _ = w_in.shape
    w_nlk = jnp.transpose(w_in, (2, 0, 1))
    shift = RET_OFF - FLOGIT_OFF
    w_main_t = pl.pallas_call(
        functools.partial(_prep_main_kernel, first_shifted=FLOGIT_OFF // rows, shift=shift),
        grid=(MAIN_WIDTH // rows,),
        in_specs=[
            pl.BlockSpec((rows, depth, d), lambda i: (i, 0, 0)),
            pl.BlockSpec((tail, depth, d), lambda i: ((i + 1) * (rows // tail), 0, 0)),
        ],
        out_specs=pl.BlockSpec((depth, rows, d), lambda i: (0, i, 0)),
        out_shape=jax.ShapeDtypeStruct((depth, MAIN_WIDTH, d), BF16),
        compiler_params=_params("parallel"),
        name="prep_in_proj_weights",
    )(w_nlk, w_nlk)
    w_f_t = pl.pallas_call(
        functools.partial(_prep_flogit_kernel, count=FOX_HEADS),
        grid=(1,),
        in_specs=[pl.BlockSpec((tail, depth, d), lambda i: (FLOGIT_OFF // tail, 0, 0))],
        out_specs=pl.BlockSpec((depth, LANE, d), lambda i: (0, 0, 0)),
        out_shape=jax.ShapeDtypeStruct((depth, LANE, d), BF16),
        compiler_params=_params("arbitrary"),
        name="prep_forget_logit_weights",
    )(w_nlk)
    return w_main_t, w_f_t


def _mm_nt_kernel(a_ref, wt_ref, o_ref):
    acc = lax.dot_general(a_ref[...], wt_ref[...], (((1,), (1,)), ((), ())),
                          preferred_element_type=F32)
    o_ref[...] = acc.astype(o_ref.dtype)


def _matmul_nt(a, wt, layer, *, tm, tn, out_dtype, name):
    m, kdim = a.shape
    n = wt.shape[1]
    return pl.pallas_call(
        _mm_nt_kernel,
        grid=(m // tm, n // tn),
        in_specs=[
            pl.BlockSpec((tm, kdim), lambda i, j: (i, 0)),
            pl.BlockSpec((None, tn, kdim), lambda i, j: (layer, j, 0)),
        ],
        out_specs=pl.BlockSpec((tm, tn), lambda i, j: (i, j)),
        out_shape=jax.ShapeDtypeStruct((m, n), out_dtype),
        compiler_params=_params("parallel", "parallel"),
        name=name,
    )(a, wt)


def _mm_ksplit_kernel(a_ref, w_ref, o_ref, acc_ref, *, nk):
    k = pl.program_id(2)

    @pl.when(k == 0)
    def _():
        acc_ref[...] = jnp.dot(a_ref[...], w_ref[...], preferred_element_type=F32)

    @pl.when(k > 0)
    def _():
        acc_ref[...] += jnp.dot(a_ref[...], w_ref[...], preferred_element_type=F32)

    @pl.when(k == nk - 1)
    def _():
        o_ref[...] = acc_ref[...].astype(o_ref.dtype)


def _matmul_ksplit(a, w, *, tm, tn, tk, out_dtype, name):
    m, kdim = a.shape
    n = w.shape[1]
    nk = kdim // tk
    assert nk >= 2
    return pl.pallas_call(
        functools.partial(_mm_ksplit_kernel, nk=nk),
        grid=(m // tm, n // tn, nk),
        in_specs=[
            pl.BlockSpec((tm, tk), lambda i, j, k: (i, k)),
            pl.BlockSpec((tk, tn), lambda i, j, k: (k, j)),
        ],
        out_specs=pl.BlockSpec((tm, tn), lambda i, j, k: (i, j)),
        out_shape=jax.ShapeDtypeStruct((m, n), out_dtype),
        scratch_shapes=[pltpu.VMEM((tm, tn), F32)],
        compiler_params=_params("parallel", "parallel", "arbitrary"),
        name=name,
    )(a, w)


def _mlp_up_kernel(h_ref, w_ref, wd_ref, *rest, ada_steps):
    if ada_steps:
        c_ref, wa_ref, ba_ref, a_ref, wd_bf16_ref, mod_ref = rest
    else:
        a_ref, wd_bf16_ref = rest
    acc = jnp.dot(h_ref[...], w_ref[...].astype(BF16), preferred_element_type=F32)
    a_ref[...] = jnp.square(jnp.maximum(acc, 0.0)).astype(a_ref.dtype)
    wd_bf16_ref[...] = wd_ref[...].astype(wd_bf16_ref.dtype)
    if ada_steps:
        step = pl.program_id(0) * pl.num_programs(1) + pl.program_id(1)

        @pl.when(step < ada_steps)
        def _():
            mod_ref[...] = _ada_block(c_ref, wa_ref, ba_ref)


def _mlp_up(h, w_up, w_down, layer, ada_next=None, tm=2048, tn=512, ada_tn=256):
    m, d = h.shape
    d_ff = w_up.shape[-1]
    n_i, n_j = m // tm, d_ff // tn
    slab = d_ff // (n_i * n_j)
    in_specs = [
        pl.BlockSpec((tm, d), lambda i, j: (i, 0), pipeline_mode=pl.Buffered(1)),
        pl.BlockSpec((None, d, tn), lambda i, j: (layer, 0, j)),
        pl.BlockSpec((None, slab, d), lambda i, j: (layer, i * n_j + j, 0)),
    ]
    out_specs = [
        pl.BlockSpec((tm, tn), lambda i, j: (i, j)),
        pl.BlockSpec((slab, d), lambda i, j: (i * n_j + j, 0)),
    ]
    out_shape = [jax.ShapeDtypeStruct((m, d_ff), BF16), jax.ShapeDtypeStruct((d_ff, d), BF16)]
    args = [h, w_up, w_down]
    ada_steps = 0
    if ada_next is not None:
        c_pad, w_ada, b_ada3 = ada_next
        rows, n_mod = c_pad.shape[0], w_ada.shape[-1]
        ada_steps = n_mod // ada_tn
        assert ada_steps <= n_i * n_j

        def ada_block(i, j):
            return jnp.minimum(i * n_j + j, ada_steps - 1)

        in_specs += [
            pl.BlockSpec((rows, d), lambda i, j: (0, 0)),
            pl.BlockSpec((None, d, ada_tn), lambda i, j: (layer + 1, 0, ada_block(i, j))),
            pl.BlockSpec((None, 1, ada_tn), lambda i, j: (layer + 1, 0, ada_block(i, j))),
        ]
        out_specs.append(pl.BlockSpec((rows, ada_tn), lambda i, j: (0, ada_block(i, j))))
        out_shape.append(jax.ShapeDtypeStruct((rows, n_mod), F32))
        args += [c_pad, w_ada, b_ada3]
    res = pl.pallas_call(
        functools.partial(_mlp_up_kernel, ada_steps=ada_steps),
        grid=(n_i, n_j),
        in_specs=in_specs,
        out_specs=out_specs,
        out_shape=out_shape,
        compiler_params=_params("arbitrary", "arbitrary") if ada_steps else _params("parallel", "parallel"),
        name="mlp_up",
    )(*args)
    return (res[0], res[1], res[2]) if ada_steps else (res[0], res[1], None)


def _out_proj_kernel(yc_ref, yf_ref, yr_ref, w_ref, o_ref):
    f0 = yc_ref.shape[1]
    r0 = f0 + yf_ref.shape[1]
    acc = jnp.dot(yc_ref[...], w_ref[0:f0, :].astype(BF16), preferred_element_type=F32)
    acc += jnp.dot(yf_ref[...], w_ref[f0:r0, :].astype(BF16), preferred_element_type=F32)
    acc += jnp.dot(yr_ref[...], w_ref[r0:, :].astype(BF16), preferred_element_type=F32)
    o_ref[...] = acc.astype(o_ref.dtype)


def _out_proj(y_conv, y_fox, y_ret, w, layer, tm=2048, tn=512):
    m = y_conv.shape[0]
    _, kdim, n = w.shape

    def lhs(width):
        return pl.BlockSpec((tm, width), lambda i, j: (i, 0), pipeline_mode=pl.Buffered(1))

    return pl.pallas_call(
        _out_proj_kernel,
        grid=(m // tm, n // tn),
        in_specs=[lhs(y_conv.shape[1]), lhs(y_fox.shape[1]), lhs(y_ret.shape[1]),
                  pl.BlockSpec((None, kdim, tn), lambda i, j: (layer, 0, j))],
        out_specs=pl.BlockSpec((tm, tn), lambda i, j: (i, j)),
        out_shape=jax.ShapeDtypeStruct((m, n), BF16),
        compiler_params=_params("parallel", "parallel"),
        name="out_proj",
    )(y_conv, y_fox, y_ret, w)


def _conv_kernel(u_ref, b_ref, c_ref, w_ref, o_ref, *, rows):
    seq, tc = u_ref.shape
    w0 = w_ref[0:1, :]
    w1 = w_ref[1:2, :]
    w2 = w_ref[2:3, :]
    rid = lax.broadcasted_iota(jnp.int32, (rows, 1), 0)

    def chunk(ci, carry):
        prev2, prev1 = carry
        r0 = pl.multiple_of(ci * rows, rows)
        z = c_ref[pl.ds(r0, rows), :].astype(F32) * u_ref[pl.ds(r0, rows), :].astype(F32)
        z1 = jnp.where(rid == 0, prev1, pltpu.roll(z, 1, 0))
        z2 = jnp.where(rid == 0, prev2, jnp.where(rid == 1, prev1, pltpu.roll(z, 2, 0)))
        y = w0 * z2 + w1 * z1 + w2 * z
        o_ref[pl.ds(r0, rows), :] = (b_ref[pl.ds(r0, rows), :].astype(F32) * y).astype(o_ref.dtype)
        return z[rows - 2:rows - 1, :], z[rows - 1:rows, :]

    zero = jnp.zeros((1, tc), F32)
    lax.fori_loop(0, seq // rows, chunk, (zero, zero))


def _short_conv(proj, conv_w, batch, seq, tc=256, rows=512):
    nblk = CONV_WIDTH // tc
    return pl.pallas_call(
        functools.partial(_conv_kernel, rows=rows),
        grid=(batch, nblk),
        in_specs=[
            pl.BlockSpec((seq, tc), lambda b, j: (b, j)),
            pl.BlockSpec((seq, tc), lambda b, j: (b, nblk + j)),
            pl.BlockSpec((seq, tc), lambda b, j: (b, 2 * nblk + j)),
            pl.BlockSpec((CONV_K, tc), lambda b, j: (0, j)),
        ],
        out_specs=pl.BlockSpec((seq, tc), lambda b, j: (b, j)),
        out_shape=jax.ShapeDtypeStruct((batch * seq, CONV_WIDTH), BF16),
        compiler_params=_params("parallel", "parallel"),
        name="short_conv",
    )(proj, proj, proj, conv_w)


def _cum_kernel(f_ref, bf_ref, o_ref, carry_ref):
    rows = f_ref.shape[0]

    @pl.when(pl.program_id(1) == 0)
    def _():
        carry_ref[...] = jnp.zeros_like(carry_ref)

    z = f_ref[...] + bf_ref[...]
    logf = jnp.minimum(z, 0.0) - jnp.log1p(jnp.exp(-jnp.abs(z)))
    p1 = logf.astype(BF16)
    r1 = logf - p1.astype(F32)
    p2 = r1.astype(BF16)
    p3 = (r1 - p2.astype(F32)).astype(BF16)
    ri = lax.broadcasted_iota(jnp.int32, (rows, rows), 0)
    ci = lax.broadcasted_iota(jnp.int32, (rows, rows), 1)
    tri = (ri >= ci).astype(BF16)
    cum = (jnp.dot(tri, p1, preferred_element_type=F32)
           + jnp.dot(tri, p2, preferred_element_type=F32)
           + jnp.dot(tri, p3, preferred_element_type=F32)) + carry_ref[...]
    o_ref[...] = cum.T
    carry_ref[...] = cum[rows - 1:rows, :]


def _forget_cumsum(f_logit, b_f_pad, batch, seq, rows):
    per_b = seq // rows
    return pl.pallas_call(
        _cum_kernel,
        grid=(batch, per_b),
        in_specs=[
            pl.BlockSpec((rows, LANE), lambda b, j: (b * per_b + j, 0)),
            pl.BlockSpec((1, LANE), lambda b, j: (0, 0)),
        ],
        out_specs=pl.BlockSpec((None, None, LANE, rows), lambda b, j: (b, j, 0, 0)),
        out_shape=jax.ShapeDtypeStruct((batch, per_b, LANE, rows), F32),
        scratch_shapes=[pltpu.VMEM((1, LANE), F32)],
        compiler_params=_params("parallel", "arbitrary"),
        name="forget_cumsum",
    )(f_logit, b_f_pad)


def _fox_kernel(q_ref, k_ref, v_ref, f_ref, o_ref, vt_ref, frep_ref, m_ref, acc_ref, *, tq, tk, hp):
    seq = q_ref.shape[0]
    inv_scale = HEAD_DIM ** 0.5
    exp2_scale = HEAD_DIM ** -0.5 * LOG2_E
    head0 = pl.program_id(1) * hp
    per_q = tq // tk
    rel_pos = (lax.broadcasted_iota(jnp.int32, (tk, tq), 1)
               - lax.broadcasted_iota(jnp.int32, (tk, tq), 0))

    def head_cols(hh):
        return slice(hh * HEAD_DIM, (hh + 1) * HEAD_DIM)

    for hh in range(hp):
        def prepare(j, carry, hh=hh):
            r0 = pl.multiple_of(j * tk, tk)
            vt_ref[hh, j, :HEAD_DIM] = v_ref[pl.ds(r0, tk), head_cols(hh)].astype(F32).T.astype(BF16)
            vt_ref[hh, j, HEAD_DIM:] = (
                lax.broadcasted_iota(jnp.int32, (ONES_ROWS, tk), 0) == 0).astype(BF16)
            f_row = f_ref[j, pl.ds(head0 + hh, 1), :] * inv_scale
            frep_ref[hh, j] = jnp.broadcast_to(f_row, (LANE, tk)).T
            return carry

        lax.fori_loop(0, seq // tk, prepare, 0)

    def q_block(qi, _):
        q0 = pl.multiple_of(qi * tq, tq)
        m_ref[...] = jnp.full_like(m_ref, -jnp.inf)
        acc_ref[...] = jnp.zeros_like(acc_ref)

        def kv_block(ki, diag):
            k0 = pl.multiple_of(ki * tk, tk)
            sub = tq // FOX_QSPLIT
            units = [(hh, c) for hh in range(hp) for c in range(FOX_QSPLIT)]
            scores = []
            for hh, c in units:
                q = q_ref[pl.ds(q0 + c * sub, sub), head_cols(hh)]
                k = k_ref[pl.ds(k0, tk), head_cols(hh)]
                scores.append(lax.dot_general(k, q, (((1,), (1,)), ((), ())),
                                              preferred_element_type=F32))
            for (hh, c), st in zip(units, scores):
                cols = slice(c * sub, (c + 1) * sub)
                u = st - jnp.concatenate([frep_ref[hh, ki]] * (sub // LANE), axis=1)
                if diag is not None:
                    u = jnp.where(rel_pos[:, cols] >= diag * tk, u, -jnp.inf)
                m_old = m_ref[hh, :, cols]
                m_new = jnp.maximum(m_old, jnp.max(u, axis=0, keepdims=True))
                p = jnp.exp2((u - m_new) * exp2_scale)
                alpha = jnp.exp2((m_old - m_new) * exp2_scale)
                acc_ref[hh, :, cols] = alpha * acc_ref[hh, :, cols] + jnp.dot(
                    vt_ref[hh, ki], p.astype(BF16), preferred_element_type=F32)
                m_ref[hh, :, cols] = m_new

        def full_block(ki, carry):
            kv_block(ki, None)
            return carry

        lax.fori_loop(0, qi * per_q, full_block, 0)
        for d in range(per_q):
            kv_block(qi * per_q + d, d)
        for hh in range(hp):
            out_t = acc_ref[hh, :HEAD_DIM] / acc_ref[hh, HEAD_DIM:HEAD_DIM + 1]
            o_ref[pl.ds(q0, tq), head_cols(hh)] = out_t.T.astype(o_ref.dtype)
        return 0

    lax.fori_loop(0, seq // tq, q_block, 0)


def _forgetting_attention(proj, cum, batch, seq, tq, tk, hp):
    groups = FOX_HEADS // hp
    width = hp * HEAD_DIM
    qoff = FOX_OFF // width
    fox_blocks = FOX_WIDTH // width
    head_rows = 16

    def head_spec(off):
        return pl.BlockSpec((seq, width), lambda b, g: (b, off + g))

    return pl.pallas_call(
        functools.partial(_fox_kernel, tq=tq, tk=tk, hp=hp),
        grid=(batch, groups),
        in_specs=[
            head_spec(qoff), head_spec(qoff + fox_blocks), head_spec(qoff + 2 * fox_blocks),
            pl.BlockSpec((None, seq // tk, head_rows, tk), lambda b, g: (b, 0, 0, 0)),
        ],
        out_specs=pl.BlockSpec((seq, width), lambda b, g: (b, g)),
        out_shape=jax.ShapeDtypeStruct((batch * seq, FOX_WIDTH), BF16),
        scratch_shapes=[
            pltpu.VMEM((hp, seq // tk, HEAD_DIM + ONES_ROWS, tk), BF16),
            pltpu.VMEM((hp, seq // tk, tk, LANE), F32),
            pltpu.VMEM((hp, 1, tq), F32),
            pltpu.VMEM((hp, HEAD_DIM + ONES_ROWS, tq), F32),
        ],
        compiler_params=_params("parallel", "parallel"),
        name="forgetting_attention",
    )(proj, proj, proj, cum)


def _ret_kernel(lg_ref, q_ref, k_ref, v_ref, g_ref, cos_ref, sin_ref, o_ref, *, chunk, hp):
    seq = q_ref.shape[0]
    scale = HEAD_DIM ** -0.5
    half = HEAD_DIM // 2
    ri = lax.broadcasted_iota(jnp.int32, (chunk, chunk), 0)
    ci = lax.broadcasted_iota(jnp.int32, (chunk, chunk), 1)
    diff = (ri - ci).astype(F32)
    jv = lax.broadcasted_iota(jnp.int32, (chunk, 1), 0).astype(F32)
    decays = []
    for hh in range(hp):
        lg = lg_ref[pl.program_id(1) * hp + hh]
        decays.append((
            jnp.where(diff >= 0.0, jnp.exp(jnp.maximum(diff, 0.0) * lg), 0.0),
            jnp.exp((jv + 1.0) * lg),
            jnp.exp((chunk - 1.0 - jv) * lg),
            jnp.exp(jnp.full((1, 1), float(chunk), F32) * lg),
        ))

    def rotate(t, cosf, sinf):
        return t * cosf + pltpu.roll(t, half, 1) * sinf

    def step(c, states):
        r0 = pl.multiple_of(c * chunk, chunk)
        rows = pl.ds(r0, chunk)
        cosf = cos_ref[rows, :]
        sinf = sin_ref[rows, :]
        new_states = []
        staged = []
        for hh in range(hp):
            cols = slice(hh * HEAD_DIM, (hh + 1) * HEAD_DIM)
            q = rotate(q_ref[rows, cols].astype(F32), cosf, sinf)
            k = rotate(k_ref[rows, cols].astype(F32), cosf, sinf) * scale
            scores = lax.dot_general(q.astype(BF16), k.astype(BF16), (((1,), (1,)), ((), ())),
                                     preferred_element_type=F32)
            staged.append((q, k, scores))
        for hh in range(hp):
            cols = slice(hh * HEAD_DIM, (hh + 1) * HEAD_DIM)
            intra, q_decay, k_decay, chunk_decay = decays[hh]
            state = states[hh]
            q, k, scores = staged[hh]
            v = v_ref[rows, cols]
            out = (jnp.dot((scores * intra).astype(BF16), v, preferred_element_type=F32)
                   + jnp.dot((q * q_decay).astype(BF16), state.astype(BF16), preferred_element_type=F32))
            new_states.append(state * chunk_decay + lax.dot_general(
                (k * k_decay).astype(BF16), v, (((0,), (0,)), ((), ())), preferred_element_type=F32))
            mu = jnp.mean(out, axis=1, keepdims=True)
            cen = out - mu
            var = jnp.mean(cen * cen, axis=1, keepdims=True)
            y = cen * lax.rsqrt(var + GN_EPS)
            g = g_ref[rows, cols].astype(F32)
            o_ref[rows, cols] = (y * (g / (1.0 + jnp.exp(-g)))).astype(o_ref.dtype)
        return tuple(new_states)

    zero = jnp.zeros((HEAD_DIM, HEAD_DIM), F32)
    lax.fori_loop(0, seq // chunk, step, (zero,) * hp)


def _retention(proj, log_gamma, cosf, sinf, batch, seq, chunk=256, hp=4):
    groups = RET_HEADS // hp
    width = hp * HEAD_DIM
    qoff = RET_MAIN_OFF // width
    ret_blocks = RET_WIDTH // width

    def head_spec(off):
        return pl.BlockSpec((seq, width), lambda b, g, lg: (b, off + g))

    table = pl.BlockSpec((seq, HEAD_DIM), lambda b, g, lg: (0, 0))
    return pl.pallas_call(
        functools.partial(_ret_kernel, chunk=chunk, hp=hp),
        grid_spec=pltpu.PrefetchScalarGridSpec(
            num_scalar_prefetch=1,
            grid=(batch, groups),
            in_specs=[head_spec(qoff), head_spec(qoff + ret_blocks), head_spec(qoff + 2 * ret_blocks),
                      head_spec(qoff + 3 * ret_blocks), table, table],
            out_specs=pl.BlockSpec((seq, width), lambda b, g, lg: (b, g)),
        ),
        out_shape=jax.ShapeDtypeStruct((batch * seq, RET_WIDTH), BF16),
        compiler_params=_params("parallel", "parallel"),
        name="retention",
    )(log_gamma, proj, proj, proj, proj, cosf, sinf)


def _ln_kernel(x_ref, y_ref, g_ref, gain_ref, bias_ref, *rest, alpha, modulate):
    r = alpha * x_ref[...] + g_ref[...] * y_ref[...].astype(F32)
    mu = jnp.mean(r, axis=1, keepdims=True)
    cen = r - mu
    var = jnp.mean(cen * cen, axis=1, keepdims=True)
    xn = cen * lax.rsqrt(var + LN_EPS) * gain_ref[...] + bias_ref[...]
    if modulate:
        sc_ref, sh_ref, xo_ref, h_ref = rest
        xo_ref[...] = xn
        h_ref[...] = (xn * (1.0 + sc_ref[...]) + sh_ref[...]).astype(h_ref.dtype)
    else:
        xo_ref, = rest
        xo_ref[...] = xn


def _residual_ln(x2d, branch, gate, gain, bias, sc, sh, seq, alpha, tm=256):
    m, d = x2d.shape
    per_b = seq // tm
    modulate = sc is not None
    row = pl.BlockSpec((tm, d), lambda i: (i, 0))
    vec_b = pl.BlockSpec((None, 1, d), lambda i: (i // per_b, 0, 0))
    vec = pl.BlockSpec((1, d), lambda i: (0, 0))
    in_specs = [row, row, vec_b, vec, vec]
    args = [x2d, branch, gate, gain, bias]
    out_specs = [row]
    out_shape = [jax.ShapeDtypeStruct((m, d), F32)]
    if modulate:
        in_specs += [vec_b, vec_b]
        args += [sc, sh]
        out_specs.append(row)
        out_shape.append(jax.ShapeDtypeStruct((m, d), BF16))
    res = pl.pallas_call(
        functools.partial(_ln_kernel, alpha=alpha, modulate=modulate),
        grid=(m // tm,),
        in_specs=in_specs,
        out_specs=out_specs,
        out_shape=out_shape,
        compiler_params=_params("parallel"),
        name="residual_layernorm",
    )(*args)
    return (res[0], res[1]) if modulate else (res[0], None)


def _rotary_tables(seq):
    pos = np.arange(seq, dtype=np.float32)
    inv_freq = (np.float32(ROPE_BASE) ** (-np.arange(0, HEAD_DIM, 2, dtype=np.float32) / np.float32(HEAD_DIM))
                ).astype(np.float32)
    ang = (pos[:, None] * inv_freq[None, :]).astype(np.float32)
    cos = np.cos(ang).astype(np.float32)
    sin = np.sin(ang).astype(np.float32)
    return (jnp.asarray(np.concatenate([cos, cos], axis=1)),
            jnp.asarray(np.concatenate([-sin, sin], axis=1)))


def _log_gamma():
    h = np.arange(RET_HEADS, dtype=np.float32)
    return jnp.asarray(np.log1p(-np.exp2(np.float32(-5.0) - h)).astype(np.float32))


def kernel(x, c, w_ada, b_ada, w_in, b_f, conv_w, w_out, ln1_g, ln1_b, w_up, w_down, ln2_g, ln2_b):
    batch, seq, d = x.shape
    depth = w_ada.shape[0]
    m = batch * seq
    alpha = (2.0 * depth) ** 0.25
    pad_rows = 8

    c_pad = jnp.zeros((pad_rows, d), F32).at[:batch].set(c)
    b_ada3 = b_ada.reshape(depth, 1, 6 * d)

    def split_mod(mod_rows):
        return mod_rows[:batch].reshape(batch, 6, 1, d)

    mod = split_mod(_ada_first_layer(c_pad, w_ada, b_ada3))

    cosf, sinf = _rotary_tables(seq)
    log_gamma = _log_gamma()

    w_main_t, w_f_t = _prep_in_proj_weights(w_in)
    b_f_pad = jnp.zeros((depth, 1, LANE), F32).at[:, 0, :FOX_HEADS].set(b_f)

    x2d = x.reshape(m, d)
    h = _modulate(x2d, mod[:, 1], mod[:, 0], seq)
    for l in range(depth):
        sh_m, sc_m, g_a, g_m = mod[:, 3], mod[:, 4], mod[:, 2], mod[:, 5]
        last = l + 1 == depth

        proj = _matmul_nt(h, w_main_t, l, tm=1024, tn=1536, out_dtype=BF16, name="in_proj")
        f_logit = _matmul_nt(h, w_f_t, l, tm=1024, tn=LANE, out_dtype=F32, name="forget_logits")

        cum = _forget_cumsum(f_logit, b_f_pad[l], batch, seq, rows=FOX_TK)
        y_conv = _short_conv(proj, conv_w[l], batch, seq)
        y_fox = _forgetting_attention(proj, cum, batch, seq, tq=FOX_TQ, tk=FOX_TK, hp=FOX_HP)
        y_ret = _retention(proj, log_gamma, cosf, sinf, batch, seq)

        mix = _out_proj(y_conv, y_fox, y_ret, w_out, l)
        x2d, h2 = _residual_ln(x2d, mix, g_a, ln1_g[l][None], ln1_b[l][None], sc_m, sh_m, seq, alpha)

        a, w_down_l, mod_next = _mlp_up(h2, w_up, w_down, l,
                                        ada_next=None if last else (c_pad, w_ada, b_ada3))
        mlp = _matmul_ksplit(a, w_down_l, tm=1024, tn=1024, tk=d, out_dtype=BF16, name="mlp_down")
        if last:
            x2d, _ = _residual_ln(x2d, mlp, g_m, ln2_g[l][None], ln2_b[l][None], None, None, seq, alpha)
        else:
            mod = split_mod(mod_next)
            x2d, h = _residual_ln(x2d, mlp, g_m, ln2_g[l][None], ln2_b[l][None],
                                  mod[:, 1], mod[:, 0], seq, alpha)
    return x2d.reshape(batch, seq, d)
```
